```python
import jax, jax.numpy as jnp
from jax import lax
import numpy as np

D_MODEL = 1024
BATCH = 4
SEQ = 4096
DEPTH = 4
DEC_BATCH = 32
DEC_SEQ = 1
PAST_LEN = 8192
PAGE_SIZE = 128

N_AB = (DEPTH + 1) // 2
N_C = DEPTH // 2
D_A = D_MODEL // 2
DK_A = 128
H_A = D_A // DK_A
DV_A = D_A // H_A
CHUNK_A = 64
D_B = D_MODEL // 2
HD_B = 64
H_B = D_B // HD_B
N_KV_B = 2
REP_B = H_B // N_KV_B
CMP_BLOCK = 32
CMP_STRIDE = 16
CMP_HIDDEN = 2 * HD_B
SEL_BLOCK = 64
N_SEL = 16
N_LOCAL = 2
WINDOW = 512
WIN_BLOCK = 128
QB_SEL = 64
D_C = D_MODEL
CHUNK_C = 128
G_C = 8
GC_DIM = D_C // G_C
D_FF = 4 * D_MODEL
A_COLS = 4 * D_A
KV_COLS = 2 * N_KV_B * HD_B
B_COLS = D_B + 3 * KV_COLS + 3 * H_B
AB_COLS = A_COLS + B_COLS
ATT_SCALE = HD_B ** -0.5
EPS = 1e-6
BIG = 1e6

kernel_name = "hgrn2_nsa_gmlp_hybrid_step"


def rmsnorm(x, g):
    xf = x.astype(jnp.float32)
    y = xf * lax.rsqrt(jnp.mean(xf * xf, axis=-1, keepdims=True) + EPS)
    return (y * g.astype(jnp.float32)).astype(x.dtype)


def layernorm(x, g, b):
    xf = x.astype(jnp.float32)
    mu = jnp.mean(xf, axis=-1, keepdims=True)
    var = jnp.mean(jnp.square(xf - mu), axis=-1, keepdims=True)
    y = (xf - mu) * lax.rsqrt(var + EPS) * g.astype(jnp.float32) + b.astype(jnp.float32)
    return y.astype(x.dtype)


def masked_softmax(s, mask):
    s = jnp.where(mask, s.astype(jnp.float32), -jnp.inf)
    m = jnp.max(s, axis=-1, keepdims=True)
    m = jnp.where(jnp.isfinite(m), m, 0.0)
    e = jnp.exp(s - m)
    return e / jnp.maximum(jnp.sum(e, axis=-1, keepdims=True), 1e-30)


def ffn(h, w1, w2):
    return jnp.square(jax.nn.relu(h @ w1)) @ w2


def hgrn2_chunked(q, k, v, logf, s0):
    B, T, H = q.shape[:3]
    c = min(CHUNK_A, T)
    pad = (-T) % c
    nc = (T + pad) // c

    def prep(a):
        a = jnp.pad(a.astype(jnp.float32), ((0, 0), (0, pad), (0, 0), (0, 0)))
        return a.reshape(B, nc, c, H, a.shape[-1]).transpose(1, 0, 3, 2, 4)

    qc, kc, vc, fc = prep(q), prep(k), prep(v), prep(logf)
    causal = jnp.tril(jnp.ones((c, c), dtype=bool))

    def step(S, inp):
        qi, ki, vi, fi = inp
        b = jnp.cumsum(fi, axis=2)
        o_inter = jnp.einsum('bhtd,bhde->bhte', qi * jnp.exp(b), S)
        diff = b[:, :, :, None, :] - b[:, :, None, :, :]
        decay = jnp.exp(jnp.where(causal[:, :, None], diff, -jnp.inf))
        att = jnp.einsum('bhtd,bhsd,bhtsd->bhts', qi, ki, decay)
        o = o_inter + jnp.einsum('bhts,bhse->bhte', att, vi)
        b_last = b[:, :, -1:, :]
        S = jnp.exp(b_last[:, :, 0, :, None]) * S + jnp.einsum('bhsd,bhse->bhde', ki * jnp.exp(b_last - b), vi)
        return S, o

    S, o = lax.scan(step, s0.astype(jnp.float32), (qc, kc, vc, fc))
    o = o.transpose(1, 0, 3, 2, 4).reshape(B, nc * c, H, v.shape[-1])[:, :T]
    return o, S


def hgrn2_mixer(za, s0, lb, norm_g):
    B, T = za.shape[:2]
    q, f, i, g = jnp.split(za.astype(jnp.float32), 4, axis=-1)
    lbf = lb.astype(jnp.float32)
    fgate = lbf + (1.0 - lbf) * jax.nn.sigmoid(f)
    heads = lambda a: a.reshape(B, T, H_A, -1)
    o, S = hgrn2_chunked(heads(jax.nn.silu(q)), heads(1.0 - fgate), heads(i), heads(jnp.log(fgate)), s0)
    o = rmsnorm(o, norm_g) * heads(jax.nn.silu(g))
    return o.reshape(B, T, D_A), S


def split_ab(z):
    B, T = z.shape[:2]
    o = A_COLS
    za = z[..., :o]
    q = z[..., o:o + D_B].reshape(B, T, N_KV_B, REP_B, HD_B)
    o += D_B
    kv_cmp = z[..., o:o + KV_COLS].reshape(B, T, 2, N_KV_B, HD_B)
    o += KV_COLS
    kv_sel = z[..., o:o + KV_COLS].reshape(B, T, 2, N_KV_B, HD_B)
    o += KV_COLS
    kv_win = z[..., o:o + KV_COLS].reshape(B, T, 2, N_KV_B, HD_B)
    o += KV_COLS
    gates = jax.nn.sigmoid(z[..., o:].astype(jnp.float32)).reshape(B, T, N_KV_B, REP_B, 3)
    return za, q, kv_cmp, kv_sel, kv_win, gates


def compress_blocks(kv, pe, w1, w2):
    B, L = kv.shape[:2]
    nseg = L // CMP_STRIDE
    r = CMP_BLOCK // CMP_STRIDE
    n_cmp = nseg - r + 1
    segs = kv[:, :nseg * CMP_STRIDE].reshape(B, nseg, CMP_STRIDE, 2, N_KV_B, HD_B)
    blocks = jnp.concatenate([segs[:, j:j + n_cmp] for j in range(r)], axis=2)
    blocks = blocks + pe.transpose(1, 0, 2)[:, :, None, :]
    flat = blocks.transpose(0, 1, 3, 4, 2, 5).reshape(B, n_cmp, 2, N_KV_B, CMP_BLOCK * HD_B)
    hid = jax.nn.gelu(jnp.einsum('bnkgf,kfh->bnkgh', flat, w1))
    return jnp.einsum('bnkgh,khd->bnkgd', hid, w2)


def nsa_cmp_sel(q, kv_cmp, kv_sel, q_pos, pe, w1, w2):
    B, T = q.shape[:2]
    L = kv_cmp.shape[1]
    qf = q.astype(jnp.float32)
    kc = compress_blocks(kv_cmp, pe, w1, w2).astype(jnp.float32)
    n_cmp = kc.shape[1]
    cstart = jnp.arange(n_cmp) * CMP_STRIDE
    cmask = (cstart + CMP_BLOCK - 1)[None, :] <= q_pos[:, None]
    s = jnp.einsum('btgrd,bngd->bgrtn', qf, kc[:, :, 0]) * ATT_SCALE
    p = masked_softmax(s, cmask)
    o_cmp = jnp.einsum('bgrtn,bngd->btgrd', p, kc[:, :, 1])
    n_sel = -(-L // SEL_BLOCK)
    sstart = jnp.arange(n_sel) * SEL_BLOCK
    overlap = ((cstart[:, None] < sstart[None, :] + SEL_BLOCK) &
               (cstart[:, None] + CMP_BLOCK > sstart[None, :])).astype(jnp.float32)
    imp = jnp.einsum('bgrtn,nj->bgtj', p, overlap)
    pad = n_sel * SEL_BLOCK - L
    ks = jnp.pad(kv_sel.astype(jnp.float32), ((0, 0), (0, pad), (0, 0), (0, 0), (0, 0)))
    ks = ks.reshape(B, n_sel, SEL_BLOCK, 2, N_KV_B, HD_B).transpose(0, 4, 1, 2, 3, 5)
    k_top = min(N_SEL, n_sel)
    qb = QB_SEL if T % QB_SEL == 0 else T
    nqb = T // qb
    bi = jnp.arange(B)[:, None, None, None]
    gi = jnp.arange(N_KV_B)[None, :, None, None]
    jsel = jnp.arange(n_sel)

    def sel_block(args):
        qi, impi, posi = args
        cur = posi // SEL_BLOCK
        valid = jsel[None, :] <= cur[:, None]
        dist = cur[:, None] - jsel[None, :]
        forced = (jsel[None, :] == 0) | ((dist >= 0) & (dist < N_LOCAL))
        rank = jnp.where(valid, impi + BIG * forced.astype(jnp.float32), -BIG)
        _, idx = lax.top_k(rank, k_top)
        kvg = ks[bi, gi, idx].reshape(B, N_KV_B, qb, k_top * SEL_BLOCK, 2, HD_B)
        kpos = idx[..., None] * SEL_BLOCK + jnp.arange(SEL_BLOCK)
        kmask = (kpos <= posi[None, None, :, None, None]).reshape(B, N_KV_B, qb, k_top * SEL_BLOCK)
        sc = jnp.einsum('btgrd,bgtkd->bgrtk', qi, kvg[..., 0, :]) * ATT_SCALE
        pr = masked_softmax(sc, kmask[:, :, None])
        return jnp.einsum('bgrtk,bgtkd->btgrd', pr, kvg[..., 1, :])

    q_blocks = qf.reshape(B, nqb, qb, N_KV_B, REP_B, HD_B).transpose(1, 0, 2, 3, 4, 5)
    imp_blocks = imp.reshape(B, N_KV_B, nqb, qb, n_sel).transpose(2, 0, 1, 3, 4)
    o_sel = lax.map(sel_block, (q_blocks, imp_blocks, q_pos.reshape(nqb, qb)))
    o_sel = o_sel.transpose(1, 0, 2, 3, 4, 5).reshape(B, T, N_KV_B, REP_B, HD_B)
    return o_cmp, o_sel


def window_attn_prompt(q, kv):
    B, S = q.shape[:2]
    nb = S // WIN_BLOCK
    nback = WINDOW // WIN_BLOCK
    kb_len = (nback + 1) * WIN_BLOCK
    kvp = jnp.pad(kv.astype(jnp.float32), ((0, 0), (WINDOW, 0), (0, 0), (0, 0), (0, 0)))
    band = jnp.concatenate([kvp[:, j * WIN_BLOCK:j * WIN_BLOCK + S].reshape(B, nb, WIN_BLOCK, 2, N_KV_B, HD_B)
                            for j in range(nback + 1)], axis=2)
    qpos = jnp.arange(S).reshape(nb, WIN_BLOCK)
    kpos = jnp.arange(nb)[:, None] * WIN_BLOCK - WINDOW + jnp.arange(kb_len)[None, :]
    d = qpos[:, :, None] - kpos[:, None, :]
    mask = (kpos[:, None, :] >= 0) & (d >= 0) & (d < WINDOW)
    qblk = q.astype(jnp.float32).reshape(B, nb, WIN_BLOCK, N_KV_B, REP_B, HD_B)
    s = jnp.einsum('bctgrd,bckgd->bcgrtk', qblk, band[:, :, :, 0]) * ATT_SCALE
    p = masked_softmax(s, mask[None, :, None, None])
    o = jnp.einsum('bcgrtk,bckgd->bctgrd', p, band[:, :, :, 1])
    return o.reshape(B, S, N_KV_B, REP_B, HD_B)


def window_attn_dense(q, kv, q_pos, k_pos):
    d = q_pos[:, None] - k_pos[None, :]
    mask = (d >= 0) & (d < WINDOW)
    kvf = kv.astype(jnp.float32)
    s = jnp.einsum('btgrd,bkgd->bgrtk', q.astype(jnp.float32), kvf[:, :, 0]) * ATT_SCALE
    p = masked_softmax(s, mask)
    return jnp.einsum('bgrtk,bkgd->btgrd', p, kvf[:, :, 1])


def merge_ab(o_a, o_cmp, o_sel, o_win, gates, w_out, dtype):
    B, T = o_a.shape[:2]
    o_b = gates[..., 0:1] * o_cmp + gates[..., 1:2] * o_sel + gates[..., 2:3] * o_win
    o = jnp.concatenate([o_a.astype(dtype), o_b.reshape(B, T, D_B).astype(dtype)], axis=-1)
    return o @ w_out


def ab_prompt(h, w_in, w_out, lb, hnorm, pe, w1, w2):
    B, S = h.shape[:2]
    za, q, kv_cmp, kv_sel, kv_win, gates = split_ab(h @ w_in)
    o_a, s_fin = hgrn2_mixer(za, jnp.zeros((B, H_A, DK_A, DV_A), jnp.float32), lb, hnorm)
    o_cmp, o_sel = nsa_cmp_sel(q, kv_cmp, kv_sel, jnp.arange(S), pe, w1, w2)
    o_win = window_attn_prompt(q, kv_win)
    out = merge_ab(o_a, o_cmp, o_sel, o_win, gates, w_out, h.dtype)
    return out, kv_cmp, kv_sel, kv_win[:, S - min(WINDOW, S):], s_fin


def gather_pages(cache, page_table):
    rows = cache[page_table]
    return rows.reshape(rows.shape[0], -1, *rows.shape[3:])


def ab_sample(h, page_table, cache_cmp, cache_sel, win_buf, s_hgrn, w_in, w_out, lb, hnorm, pe, w1, w2):
    T = h.shape[1]
    za, q, kv_cmp, kv_sel, kv_win, gates = split_ab(h @ w_in)
    o_a, s_new = hgrn2_mixer(za, s_hgrn, lb, hnorm)
    full_cmp = jnp.concatenate([gather_pages(cache_cmp, page_table).astype(kv_cmp.dtype), kv_cmp], axis=1)
    full_sel = jnp.concatenate([gather_pages(cache_sel, page_table).astype(kv_sel.dtype), kv_sel], axis=1)
    q_pos = PAST_LEN + jnp.arange(T)
    o_cmp, o_sel = nsa_cmp_sel(q, full_cmp, full_sel, q_pos, pe, w1, w2)
    w_buf = win_buf.shape[1]
    kv_w = jnp.concatenate([win_buf.astype(kv_win.dtype), kv_win], axis=1)
    k_pos = PAST_LEN - w_buf + jnp.arange(w_buf + T)
    o_win = window_attn_dense(q, kv_w, q_pos, k_pos)
    out = merge_ab(o_a, o_cmp, o_sel, o_win, gates, w_out, h.dtype)
    return out, kv_cmp, kv_sel, kv_w[:, T:], s_new


def chunk_mlp(h, w_in, ln_g, ln_b, w_s, b_s, w_out):
    B, T = h.shape[:2]
    z = jax.nn.gelu(h @ w_in)
    u, v = jnp.split(z, 2, axis=-1)
    v = layernorm(v, ln_g, ln_b)
    c = min(CHUNK_C, T)
    pad = (-T) % c
    vv = jnp.pad(v, ((0, 0), (0, pad), (0, 0))).reshape(B, (T + pad) // c, c, G_C, GC_DIM)
    ws = jnp.tril(w_s[:, :c, :c])
    mix = jnp.einsum('gts,bnsgc->bntgc', ws, vv) + b_s[:, :c].T[None, None, :, :, None]
    mix = mix.reshape(B, T + pad, D_C)[:, :T]
    return (u * mix) @ w_out, v


def setup_inputs(seed: int = 0) -> dict:
    key = jax.random.key(seed)
    ks = jax.random.split(key, 32)
    nrm = lambda k, shape, scale: jax.random.normal(k, shape, jnp.float32) * scale
    n_pages = PAST_LEN // PAGE_SIZE
    n_pool = (DEC_BATCH * n_pages * 5) // 4
    w_buf = min(WINDOW, PAST_LEN)
    page_table = jax.random.permutation(ks[0], n_pool)[:DEC_BATCH * n_pages].reshape(DEC_BATCH, n_pages).astype(jnp.int32)
    return {
        "x_prompt": nrm(ks[1], (BATCH, SEQ, D_MODEL), 1.0),
        "x_sample": nrm(ks[2], (DEC_BATCH, DEC_SEQ, D_MODEL), 1.0),
        "cache_cmp_kv": nrm(ks[3], (N_AB, n_pool, PAGE_SIZE, 2, N_KV_B, HD_B), 1.0),
        "cache_sel_kv": nrm(ks[4], (N_AB, n_pool, PAGE_SIZE, 2, N_KV_B, HD_B), 1.0),
        "state_win_kv": nrm(ks[5], (N_AB, DEC_BATCH, w_buf, 2, N_KV_B, HD_B), 1.0),
        "state_hgrn": nrm(ks[6], (N_AB, DEC_BATCH, H_A, DK_A, DV_A), 0.5),
        "page_table": page_table,
        "norm_mix": 1.0 + nrm(ks[7], (DEPTH, D_MODEL), 0.1),
        "norm_ffn": 1.0 + nrm(ks[8], (DEPTH, D_MODEL), 0.1),
        "norm_final": 1.0 + nrm(ks[9], (D_MODEL,), 0.1),
        "w_in_ab": nrm(ks[10], (N_AB, D_MODEL, AB_COLS), D_MODEL ** -0.5),
        "w_out_ab": nrm(ks[11], (N_AB, D_A + D_B, D_MODEL), (D_A + D_B) ** -0.5),
        "hgrn_lower_bounds": nrm(ks[12], (N_AB, D_A), 0.5),
        "hgrn_norm": 1.0 + nrm(ks[13], (N_AB, DV_A), 0.1),
        "cmp_pe": nrm(ks[14], (N_AB, 2, CMP_BLOCK, HD_B), 0.1),
        "cmp_w1": nrm(ks[15], (N_AB, 2, CMP_BLOCK * HD_B, CMP_HIDDEN), (CMP_BLOCK * HD_B) ** -0.5),
        "cmp_w2": nrm(ks[16], (N_AB, 2, CMP_HIDDEN, HD_B), CMP_HIDDEN ** -0.5),
        "w_in_c": nrm(ks[17], (N_C, D_MODEL, 2 * D_C), D_MODEL ** -0.5),
        "ln_c_g": 1.0 + nrm(ks[18], (N_C, D_C), 0.1),
        "ln_c_b": nrm(ks[19], (N_C, D_C), 0.1),
        "w_s": nrm(ks[20], (N_C, G_C, CHUNK_C, CHUNK_C), CHUNK_C ** -0.5),
        "b_s": 1.0 + nrm(ks[21], (N_C, G_C, CHUNK_C), 0.1),
        "w_out_c": nrm(ks[22], (N_C, D_C, D_MODEL), D_C ** -0.5),
        "w_ffn1": nrm(ks[23], (DEPTH, D_MODEL, D_FF), D_MODEL ** -0.5),
        "w_ffn2": nrm(ks[24], (DEPTH, D_FF, D_MODEL), D_FF ** -0.5),
    }


def reference(x_prompt, x_sample, cache_cmp_kv, cache_sel_kv, state_win_kv, state_hgrn, page_table,
              norm_mix, norm_ffn, norm_final, w_in_ab, w_out_ab, hgrn_lower_bounds, hgrn_norm,
              cmp_pe, cmp_w1, cmp_w2, w_in_c, ln_c_g, ln_c_b, w_s, b_s, w_out_c, w_ffn1, w_ffn2):
    lbs = jax.nn.softmax(hgrn_lower_bounds.astype(jnp.float32), axis=0)
    lbs = jnp.cumsum(lbs, axis=0) - lbs[0:1]
    hp, hs = x_prompt, x_sample
    cmp_p, cmp_s, sel_p, sel_s, win_p, win_s, hg_p, hg_s, cv_s = [], [], [], [], [], [], [], [], []
    for layer in range(DEPTH):
        j = layer // 2
        hpn = rmsnorm(hp, norm_mix[layer])
        hsn = rmsnorm(hs, norm_mix[layer])
        if layer % 2 == 0:
            op, kc, ksl, kw, st = ab_prompt(hpn, w_in_ab[j], w_out_ab[j], lbs[j], hgrn_norm[j],
                                            cmp_pe[j], cmp_w1[j], cmp_w2[j])
            cmp_p.append(kc); sel_p.append(ksl); win_p.append(kw); hg_p.append(st)
            osm, kc, ksl, kw, st = ab_sample(hsn, page_table, cache_cmp_kv[j], cache_sel_kv[j], state_win_kv[j],
                                             state_hgrn[j], w_in_ab[j], w_out_ab[j], lbs[j], hgrn_norm[j],
                                             cmp_pe[j], cmp_w1[j], cmp_w2[j])
            cmp_s.append(kc); sel_s.append(ksl); win_s.append(kw); hg_s.append(st)
        else:
            op, _ = chunk_mlp(hpn, w_in_c[j], ln_c_g[j], ln_c_b[j], w_s[j], b_s[j], w_out_c[j])
            osm, v_new = chunk_mlp(hsn, w_in_c[j], ln_c_g[j], ln_c_b[j], w_s[j], b_s[j], w_out_c[j])
            cv_s.append(v_new)
        hp = hp + op.astype(hp.dtype)
        hs = hs + osm.astype(hs.dtype)
        hp = hp + ffn(rmsnorm(hp, norm_ffn[layer]), w_ffn1[layer], w_ffn2[layer])
        hs = hs + ffn(rmsnorm(hs, norm_ffn[layer]), w_ffn1[layer], w_ffn2[layer])
    y_prompt = rmsnorm(hp, norm_final)
    y_sample = rmsnorm(hs, norm_final)
    return (y_prompt, y_sample, jnp.stack(cmp_p), jnp.stack(cmp_s), jnp.stack(sel_p), jnp.stack(sel_s),
            jnp.stack(win_p), jnp.stack(win_s), jnp.stack(hg_p), jnp.stack(hg_s), jnp.stack(cv_s))
```

```python
import functools

import numpy as np
import jax
import jax.numpy as jnp
from jax import lax
from jax.experimental import pallas as pl
from jax.experimental.pallas import tpu as pltpu

F32 = jnp.float32
BF16 = jnp.bfloat16
I32 = jnp.int32

D_MODEL = 1024
H_A = 4
DK_A = 128
D_A = 512
D_B = 512
HD_B = 64
N_KV_B = 2
REP_B = 4
CMP_BLOCK = 32
CMP_STRIDE = 16
CMP_HIDDEN = 128
SEL_BLOCK = 64
N_SEL = 16
N_LOCAL = 2
WINDOW = 512
PAGE_SIZE = 128
CHUNK_C = 128
G_C = 8
D_FF = 4096
KV_COLS = 2 * N_KV_B * HD_B
ATT_SCALE = HD_B ** -0.5
EPS = 1e-6
BIG = 1e6
NEG = -1e30

AB_SEGS = ((0, 2048), (2048, 2560), (2560, 2816), (2816, 3072), (3072, 3328), (3328, 3456))
AB_COLS = 3352
AB_COLS_PAD = 3456

HGRN_CHUNK = 128
LANES = 128


def _cparams(sem, vmem_mb=None):
    kw = dict(dimension_semantics=sem)
    if vmem_mb is not None:
        kw["vmem_limit_bytes"] = vmem_mb << 20
    return pltpu.CompilerParams(**kw)


def _const_spec(shape):
    return pl.BlockSpec(shape, lambda *a: (0,) * len(shape))


def _dot(a, b):
    return jnp.dot(a, b, preferred_element_type=F32)


def _dot_nt(a, b):
    return lax.dot_general(a, b, (((1,), (1,)), ((), ())), preferred_element_type=F32)


def _split3(x):
    hi = x.astype(BF16)
    r = x - hi.astype(F32)
    mid = r.astype(BF16)
    lo = (r - mid.astype(F32)).astype(BF16)
    return hi, mid, lo


def _dot_01_lhs(m01, x):
    hi, mid, lo = _split3(x)
    return _dot(m01, hi) + _dot(m01, mid) + _dot(m01, lo)


def _dot_01_rhs(x, m01):
    hi, mid, lo = _split3(x)
    return _dot(hi, m01) + _dot(mid, m01) + _dot(lo, m01)


def _rms(x, g):
    return x * lax.rsqrt(jnp.mean(x * x, axis=-1, keepdims=True) + EPS) * g


def _gelu(x):
    return 0.5 * x * (1.0 + jnp.tanh(0.7978845608028654 * (x + 0.044715 * (x * x * x))))


def _sigmoid(x):
    return 1.0 / (1.0 + jnp.exp(-x))


def _silu(x):
    return x * _sigmoid(x)


def _softmax_masked(s, mask):
    sm = jnp.where(mask, s, NEG)
    m = jnp.max(sm, axis=-1, keepdims=True)
    e = jnp.where(mask, jnp.exp(sm - m), 0.0)
    return e / jnp.maximum(jnp.sum(e, axis=-1, keepdims=True), 1e-30)


def _in_proj_body(x_ref, g_ref, w_ref, *outs):
    xn = _rms(x_ref[...], g_ref[...]).astype(BF16)
    for (lo, hi), o_ref in zip(AB_SEGS, outs):
        o_ref[...] = _dot(xn, w_ref[:, lo:hi]).astype(o_ref.dtype)


def in_proj_ab(x, g, w):
    m = x.shape[0]
    tm = min(m, 512)
    widths = [hi - lo for lo, hi in AB_SEGS]
    dtypes = [F32, BF16, F32, F32, F32, F32]
    return pl.pallas_call(
        _in_proj_body,
        grid=(m // tm,),
        in_specs=[pl.BlockSpec((tm, D_MODEL), lambda i: (i, 0)),
                  _const_spec((1, D_MODEL)),
                  _const_spec((D_MODEL, AB_COLS_PAD))],
        out_specs=[pl.BlockSpec((tm, w_), lambda i: (i, 0)) for w_ in widths],
        out_shape=[jax.ShapeDtypeStruct((m, w_), dt) for w_, dt in zip(widths, dtypes)],
        compiler_params=_cparams(("parallel",), 56),
        name="in_proj_ab",
    )(x, g.reshape(1, D_MODEL), w)


def _lower_bound(raw, j):
    m = jnp.max(raw, axis=0, keepdims=True)
    e = jnp.exp(raw - m)
    sm = e / jnp.sum(e, axis=0, keepdims=True)
    cs = sm[0:1]
    for i in range(1, j + 1):
        cs = cs + sm[i:i + 1]
    return cs - sm[0:1]


def _hgrn_consts(c):
    t = np.arange(c)[:, None]
    u = np.arange(c)[None, :]
    mats = [u <= t, u > t]
    masks = [t == u]
    for lvl in range(1, int(np.log2(c)) + 1):
        m = 1 << lvl
        half = m >> 1
        mid = (t // m) * m + half
        upper = (t % m) >= half
        mats.append((upper & (u >= mid) & (u <= t)) | (~upper & (u > t) & (u <= mid - 1)))
        masks.append(((t // m) == (u // m)) & upper & ((u % m) < half))
    return (np.concatenate(mats, 0).astype(np.float32), np.stack(masks).astype(np.float32))


def _hgrn_prompt_body(j, za_ref, lbraw_ref, hn_ref, mst_ref, lm_ref, o_ref, sfin_ref, st_ref):
    c = HGRN_CHUNK
    ci = pl.program_id(1)

    @pl.when(ci == 0)
    def _():
        st_ref[...] = jnp.zeros(st_ref.shape, F32)

    lb = _lower_bound(lbraw_ref[...], j)
    fg = lb + (1.0 - lb) * _sigmoid(za_ref[:, 512:1024])
    lf = jnp.log(fg)
    q = _silu(za_ref[:, 0:512])
    k = 1.0 - fg
    v = za_ref[:, 1024:1536]
    gg = _silu(za_ref[:, 1536:2048])
    mst = mst_ref[...]
    nlev = lm_ref.shape[0]
    for h in range(H_A):
        sl = slice(h * DK_A, (h + 1) * DK_A)
        qh, kh, vh = q[:, sl], k[:, sl], v[:, sl]
        x = jnp.exp(_dot_01_lhs(mst, lf[:, sl]))
        att = lm_ref[0] * _dot_nt(qh.astype(BF16), kh.astype(BF16))
        for lvl in range(1, nlev):
            d = x[(1 + lvl) * c:(2 + lvl) * c]
            att = att + lm_ref[lvl] * _dot_nt((qh * d).astype(BF16), (kh * d).astype(BF16))
        st = st_ref[h]
        o = _dot_nt((qh * x[0:c]).astype(BF16), st.astype(BF16)) + _dot(att.astype(BF16), vh.astype(BF16))
        kout = (kh * x[c:2 * c]).astype(BF16)
        st_new = st * x[c - 1:c] + _dot(vh.T.astype(BF16), kout)
        st_ref[h] = st_new
        on = _rms(o, hn_ref[...]) * gg[:, sl]
        o_ref[:, sl] = on.astype(o_ref.dtype)

        @pl.when(ci == pl.num_programs(1) - 1)
        def _():
            sfin_ref[h] = st_new.T


def hgrn_prompt(za, lb_raw, hnorm, j, batch):
    m = za.shape[0]
    t = m // batch
    c = HGRN_CHUNK
    nchunk = t // c
    mst, lm = _hgrn_consts(c)
    return pl.pallas_call(
        functools.partial(_hgrn_prompt_body, j),
        grid=(batch, nchunk),
        in_specs=[pl.BlockSpec((c, 2048), lambda b, i: (b * nchunk + i, 0)),
                  _const_spec(lb_raw.shape),
                  _const_spec((1, DK_A)),
                  _const_spec(mst.shape),
                  _const_spec(lm.shape)],
        out_specs=[pl.BlockSpec((c, D_A), lambda b, i: (b * nchunk + i, 0)),
                   pl.BlockSpec((None, H_A, DK_A, DK_A), lambda b, i: (b, 0, 0, 0))],
        out_shape=[jax.ShapeDtypeStruct((m, D_A), BF16),
                   jax.ShapeDtypeStruct((batch, H_A, DK_A, DK_A), F32)],
        scratch_shapes=[pltpu.VMEM((H_A, DK_A, DK_A), F32)],
        compiler_params=_cparams(("parallel", "arbitrary"), 48),
        name="hgrn_prompt",
    )(za, lb_raw, hnorm.reshape(1, DK_A), jnp.asarray(mst, BF16), jnp.asarray(lm, F32))


def _hgrn_step_body(j, bt, za_ref, s_ref, lbraw_ref, hn_ref, o_ref, snew_ref):
    lb = _lower_bound(lbraw_ref[...], j)
    fg = lb + (1.0 - lb) * _sigmoid(za_ref[:, 512:1024])
    q = _silu(za_ref[:, 0:512])
    k = 1.0 - fg
    v = za_ref[:, 1024:1536]
    gg = _silu(za_ref[:, 1536:2048])
    zpad = jnp.zeros((LANES - bt, DK_A), F32)

    def cols(a):
        return jnp.concatenate([a, zpad], axis=0).T

    for h in range(H_A):
        sl = slice(h * DK_A, (h + 1) * DK_A)
        qc, fc, kc = cols(q[:, sl]), cols(fg[:, sl]), cols(k[:, sl])
        for b in range(bt):
            s_new = fc[:, b:b + 1] * s_ref[b, h] + kc[:, b:b + 1] * v[b:b + 1, sl]
            snew_ref[b, h] = s_new
            o = jnp.sum(qc[:, b:b + 1] * s_new, axis=0, keepdims=True)
            o_ref[b:b + 1, sl] = (_rms(o, hn_ref[...]) * gg[b:b + 1, sl]).astype(o_ref.dtype)


def hgrn_step(za, state, lb_raw, hnorm, j):
    nb = za.shape[0]
    bt = 8
    return pl.pallas_call(
        functools.partial(_hgrn_step_body, j, bt),
        grid=(nb // bt,),
        in_specs=[pl.BlockSpec((bt, 2048), lambda i: (i, 0)),
                  pl.BlockSpec((bt, H_A, DK_A, DK_A), lambda i: (i, 0, 0, 0)),
                  _const_spec(lb_raw.shape),
                  _const_spec((1, DK_A))],
        out_specs=[pl.BlockSpec((bt, D_A), lambda i: (i, 0)),
                   pl.BlockSpec((bt, H_A, DK_A, DK_A), lambda i: (i, 0, 0, 0))],
        out_shape=[jax.ShapeDtypeStruct((nb, D_A), BF16),
                   jax.ShapeDtypeStruct(state.shape, F32)],
        compiler_params=_cparams(("parallel",)),
        name="hgrn_step",
    )(za, state, lb_raw, hnorm.reshape(1, DK_A))


def _compress_weights(pe, w1, w2):
    eye = jnp.eye(N_KV_B, dtype=F32)
    w1r = w1.reshape(2, CMP_BLOCK, HD_B, CMP_HIDDEN)
    w1x = jnp.einsum("krdh,gG->rkgdGh", w1r, eye).reshape(CMP_BLOCK, 2, N_KV_B * HD_B, N_KV_B * CMP_HIDDEN)
    pex = jnp.broadcast_to(pe.transpose(1, 0, 2)[:, :, None, :], (CMP_BLOCK, 2, N_KV_B, HD_B))
    pex = pex.reshape(CMP_BLOCK, 2, 1, N_KV_B * HD_B)
    w2x = jnp.einsum("khd,gG->kghGd", w2, eye).reshape(2, N_KV_B * CMP_HIDDEN, N_KV_B * HD_B)
    return w1x.astype(BF16), pex.astype(F32), w2x.astype(BF16)


def _compress_body(npp, nseg, *refs):
    pages = refs[1:1 + npp]
    w1_ref, pe_ref, w2_ref, o_ref, xk, xv, bbuf = refs[1 + npp:]
    s = pl.program_id(1)
    half = KV_COLS // 2
    for i, pg in enumerate(pages):
        row0 = pl.multiple_of((s * npp + i) * PAGE_SIZE, PAGE_SIZE)
        xk[pl.ds(row0, PAGE_SIZE), :] = pg[:, 0:half]
        xv[pl.ds(row0, PAGE_SIZE), :] = pg[:, half:KV_COLS]

    @pl.when(s == pl.num_programs(1) - 1)
    def _():
        bbuf[pl.ds(nseg, 8), :] = jnp.zeros((8, N_KV_B * CMP_HIDDEN), F32)
        for kv, xbuf in enumerate((xk, xv)):
            acc_a = jnp.zeros((nseg, N_KV_B * CMP_HIDDEN), F32)
            acc_b = jnp.zeros((nseg, N_KV_B * CMP_HIDDEN), F32)
            for r in range(CMP_STRIDE):
                xr = xbuf[pl.ds(r, nseg, stride=CMP_STRIDE), :]
                acc_a = acc_a + _dot((xr + pe_ref[r, kv]).astype(BF16), w1_ref[r, kv])
                acc_b = acc_b + _dot((xr + pe_ref[CMP_STRIDE + r, kv]).astype(BF16), w1_ref[CMP_STRIDE + r, kv])
            bbuf[pl.ds(0, nseg), :] = acc_b
            hid = _gelu(acc_a + bbuf[pl.ds(1, nseg), :])
            o_ref[:, kv * half:(kv + 1) * half] = _dot(hid.astype(BF16), w2_ref[kv])


def compress_pages(table, cache, w1x, pex, w2x, npp):
    nb, npg = table.shape
    nseg = npg * PAGE_SIZE // CMP_STRIDE
    page_specs = [pl.BlockSpec((None, PAGE_SIZE, KV_COLS),
                               functools.partial(lambda b, s, t, i: (t[b, s * npp + i], 0, 0), i=i))
                  for i in range(npp)]
    grid_spec = pltpu.PrefetchScalarGridSpec(
        num_scalar_prefetch=1,
        grid=(nb, npg // npp),
        in_specs=page_specs + [_const_spec(w1x.shape), _const_spec(pex.shape), _const_spec(w2x.shape)],
        out_specs=pl.BlockSpec((None, nseg, KV_COLS), lambda b, s, t: (b, 0, 0)),
        scratch_shapes=[pltpu.VMEM((npg * PAGE_SIZE, KV_COLS // 2), F32),
                        pltpu.VMEM((npg * PAGE_SIZE, KV_COLS // 2), F32),
                        pltpu.VMEM((nseg + 8, N_KV_B * CMP_HIDDEN), F32)],
    )
    return pl.pallas_call(
        functools.partial(_compress_body, npp, nseg),
        grid_spec=grid_spec,
        out_shape=jax.ShapeDtypeStruct((nb, nseg, KV_COLS), F32),
        compiler_params=_cparams(("parallel", "arbitrary"), 56),
        name="compress_pages",
    )(table, *([cache] * npp), w1x, pex, w2x)


def _overlap_matrix(n_cmp_pad, n_sel_pad, n_cmp, n_sel):
    cs = np.arange(n_cmp_pad)[:, None] * CMP_STRIDE
    ss = np.arange(n_sel_pad)[None, :] * SEL_BLOCK
    ov = (cs < ss + SEL_BLOCK) & (cs + CMP_BLOCK > ss)
    ov &= (np.arange(n_cmp_pad)[:, None] < n_cmp) & (np.arange(n_sel_pad)[None, :] < n_sel)
    return ov.astype(np.float32)


def _rank(imp, jidx, cur):
    valid = jidx <= cur
    dist = cur - jidx
    forced = (jidx == 0) | ((dist >= 0) & (dist < N_LOCAL))
    return jnp.where(valid, imp + BIG * forced.astype(F32), -BIG)


def _nsa_prompt_body(tq, tk, n_cmp, n_sel, q_ref, kc_ref, vc_ref, ks_ref, vs_ref, kw_ref, vw_ref,
                     gt_ref, ov_ref, ex_ref, o_ref, m_sc, l_sc, acc_sc):
    r_ = REP_B
    t0 = pl.program_id(2) * tq
    qs = q_ref[...].reshape(r_ * tq, HD_B) * ATT_SCALE
    tpos = t0 + lax.broadcasted_iota(I32, (tq, 1), 0)

    ncp = kc_ref.shape[0]
    s = _dot_nt(qs, kc_ref[...]).reshape(r_, tq, ncp)
    n_idx = lax.broadcasted_iota(I32, (1, ncp), 1)
    cmask = ((n_idx * CMP_STRIDE + (CMP_BLOCK - 1)) <= tpos) & (n_idx < n_cmp)
    p = _softmax_masked(s, cmask[None])
    o_cmp = _dot(p.reshape(r_ * tq, ncp).astype(BF16), vc_ref[...])
    psum = p[0]
    for r in range(1, r_):
        psum = psum + p[r]
    imp = _dot_01_rhs(psum, ov_ref[...])

    imp_t = imp.T[0:n_sel]
    jidx = lax.broadcasted_iota(I32, (n_sel, tq), 0)
    cur = jnp.right_shift(t0 + lax.broadcasted_iota(I32, (n_sel, tq), 1), 6)
    rank = _rank(imp_t, jidx, cur)
    cnt = jnp.zeros((n_sel, tq), I32)
    for i in range(n_sel):
        ri = rank[i:i + 1]
        beats = (ri > rank) | ((ri == rank) & (jidx > i))
        cnt = cnt + beats.astype(I32)
    sel_t = (cnt < min(N_SEL, n_sel)).astype(F32)
    if n_sel < LANES:
        sel_t = jnp.concatenate([sel_t, jnp.zeros((LANES - n_sel, tq), F32)], axis=0)
    sel = sel_t.T.astype(BF16)

    m_sc[...] = jnp.full(m_sc.shape, NEG, F32)
    l_sc[...] = jnp.zeros(l_sc.shape, F32)
    acc_sc[...] = jnp.zeros(acc_sc.shape, F32)
    nkt = (t0 + tq + tk - 1) // tk

    def kv_step(kt, carry):
        k0 = pl.multiple_of(kt * tk, tk)
        sc = _dot_nt(qs, ks_ref[pl.ds(k0, tk), :]).reshape(r_, tq, tk)
        picked = _dot(sel, ex_ref[kt])
        kpos = k0 + lax.broadcasted_iota(I32, (1, tk), 1)
        valid = ((picked > 0.5) & (kpos <= tpos))[None]
        sm = jnp.where(valid, sc, NEG)
        m_old = m_sc[...]
        m_new = jnp.maximum(m_old, jnp.max(sm, axis=-1, keepdims=True))
        alpha = jnp.exp(m_old - m_new)
        e = jnp.where(valid, jnp.exp(sm - m_new), 0.0)
        l_sc[...] = alpha * l_sc[...] + jnp.sum(e, axis=-1, keepdims=True)
        m_sc[...] = m_new
        pv = _dot(e.reshape(r_ * tq, tk).astype(BF16), vs_ref[pl.ds(k0, tk), :])
        acc_sc[...] = alpha.reshape(r_ * tq, 1) * acc_sc[...] + pv
        return carry

    lax.fori_loop(0, nkt, kv_step, 0)
    o_sel = acc_sc[...] / jnp.maximum(l_sc[...], 1e-30).reshape(r_ * tq, 1)

    kwl = WINDOW + tq
    w0 = pl.multiple_of(jnp.maximum(t0 - WINDOW, 0), tq)
    sw = _dot_nt(qs, kw_ref[pl.ds(w0, kwl), :]).reshape(r_, tq, kwl)
    dist = tpos - (w0 + lax.broadcasted_iota(I32, (1, kwl), 1))
    pw = _softmax_masked(sw, ((dist >= 0) & (dist < WINDOW))[None])
    o_win = _dot(pw.reshape(r_ * tq, kwl).astype(BF16), vw_ref[pl.ds(w0, kwl), :])

    g = _sigmoid(gt_ref[...])
    outs = []
    for r in range(r_):
        rows = slice(r * tq, (r + 1) * tq)
        outs.append(g[:, 3 * r:3 * r + 1] * o_cmp[rows] + g[:, 3 * r + 1:3 * r + 2] * o_sel[rows]
                    + g[:, 3 * r + 2:3 * r + 3] * o_win[rows])
    o_ref[...] = jnp.concatenate(outs, axis=1).astype(o_ref.dtype)


def nsa_prompt(q, kvc, kvs, kvw, gates, n_cmp):
    b_, g_, r_, t, _ = q.shape
    tq = 128
    tk = min(512, t)
    n_sel = t // SEL_BLOCK
    ncp = kvc.shape[3]
    ov = jnp.asarray(_overlap_matrix(ncp, LANES, n_cmp, n_sel), BF16)
    kb = np.arange(t) // SEL_BLOCK
    ex = (np.arange(LANES)[:, None] == kb[None, :]).astype(np.float32)
    ex = jnp.asarray(ex.reshape(LANES, t // tk, tk).transpose(1, 0, 2), BF16)
    kv_spec = lambda which, n: pl.BlockSpec((None, None, None, n, HD_B),
                                            lambda b, g, i: (b, which, g, 0, 0))
    nqt = t // tq
    return pl.pallas_call(
        functools.partial(_nsa_prompt_body, tq, tk, n_cmp, n_sel),
        grid=(b_, g_, nqt),
        in_specs=[pl.BlockSpec((None, None, r_, tq, HD_B), lambda b, g, i: (b, g, 0, i, 0)),
                  kv_spec(0, ncp), kv_spec(1, ncp),
                  kv_spec(0, t), kv_spec(1, t),
                  kv_spec(0, t), kv_spec(1, t),
                  pl.BlockSpec((None, None, tq, r_ * 3), lambda b, g, i: (b, g, i, 0)),
                  _const_spec(ov.shape), _const_spec(ex.shape)],
        out_specs=pl.BlockSpec((tq, r_ * HD_B), lambda b, g, i: (b * nqt + i, g)),
        out_shape=jax.ShapeDtypeStruct((b_ * t, g_ * r_ * HD_B), BF16),
        scratch_shapes=[pltpu.VMEM((r_, tq, 1), F32), pltpu.VMEM((r_, tq, 1), F32),
                        pltpu.VMEM((r_ * tq, HD_B), F32)],
        compiler_params=_cparams(("parallel", "parallel", "arbitrary"), 48),
        name="nsa_prompt",
    )(q, kvc, kvc, kvs, kvs, kvw, kvw, gates, ov, ex)


def _nsa_step_select_body(q_pos, n_cmp, n_sel, q_ref, kc_ref, vc_ref, ov_ref, ocmp_ref, idx_ref):
    q8 = q_ref[...] * ATT_SCALE
    ncp = kc_ref.shape[2]
    nsp = ov_ref.shape[1]
    n_idx = lax.broadcasted_iota(I32, (1, ncp), 1)
    cmask = ((n_idx * CMP_STRIDE + (CMP_BLOCK - 1)) <= q_pos) & (n_idx < n_cmp)
    head = lax.broadcasted_iota(I32, (N_KV_B * REP_B, 1), 0)
    o_all = jnp.zeros((N_KV_B * REP_B, HD_B), F32)
    for g in range(N_KV_B):
        ingrp = (head // REP_B) == g
        p = _softmax_masked(_dot_nt(q8, kc_ref[0, g]), cmask)
        o_all = jnp.where(ingrp, _dot(p.astype(BF16), vc_ref[0, g]), o_all)
        imp8 = _dot_01_rhs(jnp.where(ingrp, p, 0.0), ov_ref[...])
        imp = jnp.sum(imp8, axis=0, keepdims=True)
        lane_j = lax.broadcasted_iota(I32, (nsp, nsp), 1)
        sub_i = lax.broadcasted_iota(I32, (nsp, nsp), 0)
        cur = q_pos // SEL_BLOCK
        rank_row = jnp.broadcast_to(_rank(imp, lane_j[0:1], cur), (nsp, nsp))
        rank_col = rank_row.T
        before = (rank_row > rank_col) | ((rank_row == rank_col) & (lane_j < sub_i))
        order = jnp.sum(before.astype(F32), axis=1, keepdims=True)
        slot = lax.broadcasted_iota(I32, (nsp, LANES), 1).astype(F32)
        hit = (order == slot) & (sub_i[:, 0:1] < n_sel)
        blk = lax.broadcasted_iota(I32, (nsp, LANES), 0).astype(F32)
        idx = jnp.sum(jnp.where(hit, blk, 0.0), axis=0, keepdims=True)
        idx_ref[g:g + 1, :] = idx.astype(I32)
    ocmp_ref[...] = o_all


def nsa_step_select(q, kvc, n_cmp, n_sel, q_pos):
    nb = q.shape[0]
    ncp = kvc.shape[3]
    nsp = 256
    ov = jnp.asarray(_overlap_matrix(ncp, nsp, n_cmp, n_sel), BF16)
    kv_spec = lambda which: pl.BlockSpec((None, 1, N_KV_B, ncp, HD_B), lambda b: (b, which, 0, 0, 0))
    return pl.pallas_call(
        functools.partial(_nsa_step_select_body, q_pos, n_cmp, n_sel),
        grid=(nb,),
        in_specs=[pl.BlockSpec((None, N_KV_B * REP_B, HD_B), lambda b: (b, 0, 0)),
                  kv_spec(0), kv_spec(1), _const_spec(ov.shape)],
        out_specs=[pl.BlockSpec((None, N_KV_B * REP_B, HD_B), lambda b: (b, 0, 0)),
                   pl.BlockSpec((None, N_KV_B, LANES), lambda b: (b, 0, 0))],
        out_shape=[jax.ShapeDtypeStruct((nb, N_KV_B * REP_B, HD_B), F32),
                   jax.ShapeDtypeStruct((nb, N_KV_B, LANES), I32)],
        compiler_params=_cparams(("parallel",)),
        name="nsa_step_select",
    )(q, kvc, kvc, ov)


def _nsa_step_attend_body(q_pos, past_len, k_top, *refs):
    tbl_ref, idx_ref = refs[0], refs[1]
    blocks = refs[2:2 + N_KV_B * k_top]
    (q_ref, ksn_ref, win_ref, kwn_ref, ocmp_ref, gt_ref, o_ref, kbuf, wbuf) = refs[2 + N_KV_B * k_top:]
    b = pl.program_id(0)
    nh = N_KV_B * REP_B
    q8 = q_ref[...] * ATT_SCALE
    zero8 = jnp.zeros((nh, HD_B), q8.dtype)
    head = lax.broadcasted_iota(I32, (nh, 1), 0)
    nsel_keys = k_top * SEL_BLOCK
    wlen = win_ref.shape[0]
    wbuf[pl.ds(0, wlen), :] = win_ref[...]
    wbuf[pl.ds(wlen, LANES), :] = jnp.broadcast_to(kwn_ref[...], (LANES, KV_COLS))
    wrow = lax.broadcasted_iota(I32, (1, wlen + LANES), 1)
    wdist = q_pos - (past_len - wlen + wrow)
    wmask = (wdist >= 0) & (wdist < WINDOW) & (wrow <= wlen)
    o_sel = jnp.zeros((nh, HD_B), F32)
    o_win = jnp.zeros((nh, HD_B), F32)
    for g in range(N_KV_B):
        ingrp = (head // REP_B) == g
        qp = jnp.concatenate([q8, zero8] if g == 0 else [zero8, q8], axis=1)
        has_new = jnp.zeros((), I32)
        krow = lax.broadcasted_iota(I32, (1, nsel_keys + LANES), 1)
        kmask = jnp.zeros((1, nsel_keys + LANES), jnp.bool_)
        for i in range(k_top):
            kbuf[pl.ds(i * SEL_BLOCK, SEL_BLOCK), :] = blocks[g * k_top + i][...]
            jb = idx_ref[b, g, i]
            kpos = jb * SEL_BLOCK + (krow - i * SEL_BLOCK)
            inblk = (krow >= i * SEL_BLOCK) & (krow < (i + 1) * SEL_BLOCK)
            kmask = kmask | (inblk & (kpos <= q_pos) & (kpos < past_len))
            has_new = has_new | (jb * SEL_BLOCK + SEL_BLOCK > past_len).astype(I32)
        kbuf[pl.ds(nsel_keys, LANES), :] = jnp.broadcast_to(ksn_ref[...], (LANES, KV_COLS))
        kmask = kmask | ((krow == nsel_keys) & (has_new > 0))
        ps = _softmax_masked(_dot_nt(qp, kbuf[:, 0:LANES].astype(BF16)), kmask)
        os_ = _dot(ps.astype(BF16), kbuf[:, LANES:2 * LANES].astype(BF16))
        o_sel = jnp.where(ingrp, os_[:, g * HD_B:(g + 1) * HD_B], o_sel)
        pw = _softmax_masked(_dot_nt(qp, wbuf[:, 0:LANES].astype(BF16)), wmask)
        ow_ = _dot(pw.astype(BF16), wbuf[:, LANES:2 * LANES].astype(BF16))
        o_win = jnp.where(ingrp, ow_[:, g * HD_B:(g + 1) * HD_B], o_win)
    gt = _sigmoid(gt_ref[...])
    o_ref[...] = (gt[:, 0:1] * ocmp_ref[...] + gt[:, 1:2] * o_sel + gt[:, 2:3] * o_win).astype(o_ref.dtype)


def nsa_step_attend(table, idx, cache_sel, q, kvs_new, win_buf, kvw_new, o_cmp, gates, q_pos, past_len):
    nb = q.shape[0]
    k_top = N_SEL
    wlen = win_buf.shape[1]
    halves = PAGE_SIZE // SEL_BLOCK
    cache_blocks = cache_sel.reshape(cache_sel.shape[0] * halves, SEL_BLOCK, KV_COLS)
    last_cached = past_len // SEL_BLOCK - 1

    def blk_map(b, t, ix, g, i):
        jb = jnp.minimum(ix[b, g, i], last_cached)
        return (t[b, jb // halves] * halves + jb % halves, 0, 0)

    blk_specs = [pl.BlockSpec((None, SEL_BLOCK, KV_COLS), functools.partial(blk_map, g=g, i=i))
                 for g in range(N_KV_B) for i in range(k_top)]
    nh = N_KV_B * REP_B
    row_spec = lambda w: pl.BlockSpec((None, nh, w), lambda b, t, ix: (b, 0, 0))
    new_spec = pl.BlockSpec((None, 1, KV_COLS), lambda b, t, ix: (b, 0, 0))
    grid_spec = pltpu.PrefetchScalarGridSpec(
        num_scalar_prefetch=2,
        grid=(nb,),
        in_specs=blk_specs + [row_spec(HD_B), new_spec,
                              pl.BlockSpec((None, wlen, KV_COLS), lambda b, t, ix: (b, 0, 0)),
                              new_spec, row_spec(HD_B), row_spec(3)],
        out_specs=row_spec(HD_B),
        scratch_shapes=[pltpu.VMEM((k_top * SEL_BLOCK + LANES, KV_COLS), F32),
                        pltpu.VMEM((wlen + LANES, KV_COLS), F32)],
    )
    return pl.pallas_call(
        functools.partial(_nsa_step_attend_body, q_pos, past_len, k_top),
        grid_spec=grid_spec,
        out_shape=jax.ShapeDtypeStruct((nb, nh, HD_B), BF16),
        compiler_params=_cparams(("arbitrary",)),
        name="nsa_step_attend",
    )(table, idx, *([cache_blocks] * (N_KV_B * k_top)), q, kvs_new, win_buf, kvw_new, o_cmp, gates)


def _out_proj_body(oa_ref, ob_ref, w_ref, r_ref, o_ref):
    o_ref[...] = r_ref[...] + _dot(oa_ref[...], w_ref[0:D_A, :]) + _dot(ob_ref[...], w_ref[D_A:D_A + D_B, :])


def out_proj(o_a, o_b, w, resid):
    m = resid.shape[0]
    tm = min(m, 512)
    return pl.pallas_call(
        _out_proj_body,
        grid=(m // tm,),
        in_specs=[pl.BlockSpec((tm, D_A), lambda i: (i, 0)), pl.BlockSpec((tm, D_B), lambda i: (i, 0)),
                  _const_spec(w.shape), pl.BlockSpec((tm, D_MODEL), lambda i: (i, 0))],
        out_specs=pl.BlockSpec((tm, D_MODEL), lambda i: (i, 0)),
        out_shape=jax.ShapeDtypeStruct((m, D_MODEL), F32),
        compiler_params=_cparams(("parallel",)),
        name="out_proj",
    )(o_a, o_b, w, resid)


def _ffn_body(final, x_ref, g_ref, w1_ref, w2_ref, gf_ref, o_ref, xn_sc, acc_sc):
    jf = pl.program_id(1)

    @pl.when(jf == 0)
    def _():
        xn_sc[...] = _rms(x_ref[...], g_ref[...]).astype(BF16)
        acc_sc[...] = jnp.zeros(acc_sc.shape, F32)

    h = jnp.maximum(_dot(xn_sc[...], w1_ref[...]), 0.0)
    acc_sc[...] += _dot((h * h).astype(BF16), w2_ref[...])

    @pl.when(jf == pl.num_programs(1) - 1)
    def _():
        y = x_ref[...] + acc_sc[...]
        o_ref[...] = _rms(y, gf_ref[...]) if final else y


def ffn(x, g, w1, w2, g_final, final):
    m = x.shape[0]
    tm = min(m, 1024)
    tf = 512
    return pl.pallas_call(
        functools.partial(_ffn_body, final),
        grid=(m // tm, D_FF // tf),
        in_specs=[pl.BlockSpec((tm, D_MODEL), lambda i, j: (i, 0)),
                  _const_spec((1, D_MODEL)),
                  pl.BlockSpec((D_MODEL, tf), lambda i, j: (0, j)),
                  pl.BlockSpec((tf, D_MODEL), lambda i, j: (j, 0)),
                  _const_spec((1, D_MODEL))],
        out_specs=pl.BlockSpec((tm, D_MODEL), lambda i, j: (i, 0)),
        out_shape=jax.ShapeDtypeStruct((m, D_MODEL), F32),
        scratch_shapes=[pltpu.VMEM((tm, D_MODEL), BF16), pltpu.VMEM((tm, D_MODEL), F32)],
        compiler_params=_cparams(("parallel", "arbitrary"), 48),
        name="ffn",
    )(x, g.reshape(1, D_MODEL), w1, w2, g_final.reshape(1, D_MODEL))


def _layernorm(v, g, b):
    mu = jnp.mean(v, axis=-1, keepdims=True)
    var = jnp.mean(jnp.square(v - mu), axis=-1, keepdims=True)
    return (v - mu) * lax.rsqrt(var + EPS) * g + b


def _gmlp_prompt_body(tm, x_ref, g_ref, win_ref, lng_ref, lnb_ref, ws_ref, bs_ref, wout_ref, o_ref, mix_sc):
    x = x_ref[...]
    z = _gelu(_dot(_rms(x, g_ref[...]).astype(BF16), win_ref[...]))
    u = z[:, 0:D_MODEL]
    v = _layernorm(z[:, D_MODEL:2 * D_MODEL], lng_ref[...], lnb_ref[...]).astype(BF16)
    row = lax.broadcasted_iota(I32, (CHUNK_C, CHUNK_C), 0)
    col = lax.broadcasted_iota(I32, (CHUNK_C, CHUNK_C), 1)
    gcd = D_MODEL // G_C
    for gi in range(G_C):
        ws = jnp.where(row >= col, ws_ref[gi], 0.0).astype(BF16)
        for ch in range(tm // CHUNK_C):
            rows = slice(ch * CHUNK_C, (ch + 1) * CHUNK_C)
            cols = slice(gi * gcd, (gi + 1) * gcd)
            mix_sc[rows, cols] = _dot(ws, v[rows, cols]) + bs_ref[gi]
    o_ref[...] = x + _dot((u * mix_sc[...]).astype(BF16), wout_ref[...])


def gmlp_prompt(x, g, w_in, ln_g, ln_b, w_s, b_s, w_out):
    m = x.shape[0]
    tm = 256
    return pl.pallas_call(
        functools.partial(_gmlp_prompt_body, tm),
        grid=(m // tm,),
        in_specs=[pl.BlockSpec((tm, D_MODEL), lambda i: (i, 0)),
                  _const_spec((1, D_MODEL)), _const_spec(w_in.shape),
                  _const_spec((1, D_MODEL)), _const_spec((1, D_MODEL)),
                  _const_spec(w_s.shape), _const_spec((G_C, CHUNK_C, 1)), _const_spec(w_out.shape)],
        out_specs=pl.BlockSpec((tm, D_MODEL), lambda i: (i, 0)),
        out_shape=jax.ShapeDtypeStruct((m, D_MODEL), F32),
        scratch_shapes=[pltpu.VMEM((tm, D_MODEL), F32)],
        compiler_params=_cparams(("parallel",), 48),
        name="gmlp_prompt",
    )(x, g.reshape(1, D_MODEL), w_in, ln_g.reshape(1, D_MODEL), ln_b.reshape(1, D_MODEL),
      w_s, b_s.reshape(G_C, CHUNK_C, 1), w_out)


def _gmlp_step_body(x_ref, g_ref, win_ref, lng_ref, lnb_ref, sc_ref, bi_ref, wout_ref, o_ref, v_ref):
    x = x_ref[...]
    z = _gelu(_dot(_rms(x, g_ref[...]).astype(BF16), win_ref[...]))
    v = _layernorm(z[:, D_MODEL:2 * D_MODEL], lng_ref[...], lnb_ref[...])
    v_ref[...] = v
    mix = v * sc_ref[...] + bi_ref[...]
    o_ref[...] = x + _dot((z[:, 0:D_MODEL] * mix).astype(BF16), wout_ref[...])


def gmlp_step(x, g, w_in, ln_g, ln_b, w_s, b_s, w_out):
    m = x.shape[0]
    gcd = D_MODEL // G_C
    scale = jnp.repeat(w_s[:, 0, 0], gcd).reshape(1, D_MODEL)
    bias = jnp.repeat(b_s[:, 0], gcd).reshape(1, D_MODEL)
    row = _const_spec((1, D_MODEL))
    full = _const_spec((m, D_MODEL))
    return pl.pallas_call(
        _gmlp_step_body,
        grid=(1,),
        in_specs=[full, row, _const_spec(w_in.shape), row, row, row, row, _const_spec(w_out.shape)],
        out_specs=[full, full],
        out_shape=[jax.ShapeDtypeStruct((m, D_MODEL), F32), jax.ShapeDtypeStruct((m, D_MODEL), F32)],
        compiler_params=_cparams(("arbitrary",), 48),
        name="gmlp_step",
    )(x, g.reshape(1, D_MODEL), w_in, ln_g.reshape(1, D_MODEL), ln_b.reshape(1, D_MODEL), scale, bias, w_out)


def _heads_major(a, parts):
    b_, t = a.shape[0], a.shape[1]
    return a.reshape(b_, t, parts, N_KV_B, HD_B).transpose(0, 2, 3, 1, 4)


def kernel(x_prompt, x_sample, cache_cmp_kv, cache_sel_kv, state_win_kv, state_hgrn, page_table, norm_mix, norm_ffn, norm_final, w_in_ab, w_out_ab, hgrn_lower_bounds, hgrn_norm, cmp_pe, cmp_w1, cmp_w2, w_in_c, ln_c_g, ln_c_b, w_s, b_s, w_out_c, w_ffn1, w_ffn2):
    bp, seq, d = x_prompt.shape
    bs = x_sample.shape[0]
    depth = norm_mix.shape[0]
    n_ab = w_in_ab.shape[0]
    n_pool = cache_cmp_kv.shape[1]
    past_len = page_table.shape[1] * PAGE_SIZE
    wlen = state_win_kv.shape[2]
    hp = x_prompt.reshape(bp * seq, d)
    hs = x_sample.reshape(bs, d)

    prompt_pages = jnp.arange(bp * seq // PAGE_SIZE, dtype=I32).reshape(bp, seq // PAGE_SIZE)
    n_cmp_p = seq // CMP_STRIDE - 1
    n_cmp_s = (past_len + 1) // CMP_STRIDE - 1
    n_sel_s = -(-(past_len + 1) // SEL_BLOCK)

    cmp_p, cmp_s, sel_p, sel_s, win_p, win_s, hg_p, hg_s, cv_s = [], [], [], [], [], [], [], [], []
    for layer in range(depth):
        j = layer // 2
        if layer % 2 == 0:
            w_in = jnp.pad(w_in_ab[j], ((0, 0), (0, AB_COLS_PAD - AB_COLS))).astype(BF16)
            w_out = w_out_ab[j].astype(BF16)
            w1x, pex, w2x = _compress_weights(cmp_pe[j], cmp_w1[j], cmp_w2[j])

            za, q, kvc, kvs, kvw, gt = in_proj_ab(hp, norm_mix[layer], w_in)
            o_a, s_fin = hgrn_prompt(za, hgrn_lower_bounds, hgrn_norm[j], j, bp)
            kc = compress_pages(prompt_pages, kvc.reshape(-1, PAGE_SIZE, KV_COLS), w1x, pex, w2x,
                                npp=min(16, seq // PAGE_SIZE))
            qh = q.reshape(bp, seq, N_KV_B, REP_B, HD_B).transpose(0, 2, 3, 1, 4)
            gth = gt[:, :3 * N_KV_B * REP_B].reshape(bp, seq, N_KV_B, REP_B * 3).transpose(0, 2, 1, 3)
            o_b = nsa_prompt(qh,
                             _heads_major(kc, 2).astype(BF16),
                             _heads_major(kvs.reshape(bp, seq, KV_COLS), 2).astype(BF16),
                             _heads_major(kvw.reshape(bp, seq, KV_COLS), 2).astype(BF16),
                             gth, n_cmp_p)
            hp = out_proj(o_a, o_b, w_out, hp)
            kv6 = lambda a: a.reshape(bp, seq, 2, N_KV_B, HD_B)
            cmp_p.append(kv6(kvc)); sel_p.append(kv6(kvs)); hg_p.append(s_fin)
            win_p.append(kv6(kvw)[:, seq - min(WINDOW, seq):])

            za, q, kvc, kvs, kvw, gt = in_proj_ab(hs, norm_mix[layer], w_in)
            o_a, s_new = hgrn_step(za, state_hgrn[j], hgrn_lower_bounds, hgrn_norm[j], j)
            kc = compress_pages(page_table, cache_cmp_kv[j].reshape(n_pool, PAGE_SIZE, KV_COLS), w1x, pex, w2x,
                                npp=min(16, page_table.shape[1]))
            q8 = q.reshape(bs, N_KV_B * REP_B, HD_B)
            o_cmp, idx = nsa_step_select(q8, _heads_major(kc, 2).astype(BF16), n_cmp_s, n_sel_s, past_len)
            win_buf = state_win_kv[j].reshape(bs, wlen, KV_COLS)
            o_b = nsa_step_attend(page_table, idx, cache_sel_kv[j].reshape(n_pool, PAGE_SIZE, KV_COLS), q8,
                                  kvs.reshape(bs, 1, KV_COLS), win_buf, kvw.reshape(bs, 1, KV_COLS), o_cmp,
                                  gt[:, :3 * N_KV_B * REP_B].reshape(bs, N_KV_B * REP_B, 3), past_len, past_len)
            hs = out_proj(o_a, o_b.reshape(bs, D_B), w_out, hs)
            kv6s = lambda a: a.reshape(bs, 1, 2, N_KV_B, HD_B)
            cmp_s.append(kv6s(kvc)); sel_s.append(kv6s(kvs)); hg_s.append(s_new)
            win_s.append(jnp.concatenate([state_win_kv[j], kv6s(kvw)], axis=1)[:, 1:])
        else:
            w_in = w_in_c[j].astype(BF16)
            w_out = w_out_c[j].astype(BF16)
            hp = gmlp_prompt(hp, norm_mix[layer], w_in, ln_c_g[j], ln_c_b[j], w_s[j], b_s[j], w_out)
            hs, v_new = gmlp_step(hs, norm_mix[layer], w_in, ln_c_g[j], ln_c_b[j], w_s[j], b_s[j], w_out)
            cv_s.append(v_new.reshape(bs, 1, d))
        w1 = w_ffn1[layer].astype(BF16)
        w2 = w_ffn2[layer].astype(BF16)
        final = layer == depth - 1
        hp = ffn(hp, norm_ffn[layer], w1, w2, norm_final, final)
        hs = ffn(hs, norm_ffn[layer], w1, w2, norm_final, final)
    return (hp.reshape(bp, seq, d), hs.reshape(bs, 1, d), jnp.stack(cmp_p), jnp.stack(cmp_s),
            jnp.stack(sel_p), jnp.stack(sel_s), jnp.stack(win_p), jnp.stack(win_s),
            jnp.stack(hg_p), jnp.stack(hg_s), jnp.stack(cv_s))
```

```python
import functools

import numpy as np
import jax
import jax.numpy as jnp
from jax import lax
from jax.experimental import pallas as pl
from jax.experimental.pallas import tpu as pltpu

F32 = jnp.float32
BF16 = jnp.bfloat16
I32 = jnp.int32

D_MODEL = 1024
H_A = 4
DK_A = 128
D_A = 512
D_B = 512
HD_B = 64
N_KV_B = 2
REP_B = 4
CMP_BLOCK = 32
CMP_STRIDE = 16
CMP_HIDDEN = 128
SEL_BLOCK = 64
N_SEL = 16
N_LOCAL = 2
WINDOW = 512
PAGE_SIZE = 128
CHUNK_C = 128
G_C = 8
D_FF = 4096
KV_COLS = 2 * N_KV_B * HD_B
ATT_SCALE = HD_B ** -0.5
EPS = 1e-6
BIG = 1e6
NEG = -1e30
NEG_MASK = -3e38

AB_SEGS = ((0, 2048), (2048, 2560), (2560, 2816), (2816, 3072), (3072, 3328), (3328, 3456))
AB_COLS = 3352
AB_COLS_PAD = 3456

HGRN_CHUNK = 128
LANES = 128


def _cparams(sem, vmem_mb=None):
    kw = dict(dimension_semantics=sem)
    if vmem_mb is not None:
        kw["vmem_limit_bytes"] = vmem_mb << 20
    return pltpu.CompilerParams(**kw)


def _const_spec(shape):
    return pl.BlockSpec(shape, lambda *a: (0,) * len(shape))


def _dot(a, b):
    return jnp.dot(a, b, preferred_element_type=F32)


def _dot_nt(a, b):
    return lax.dot_general(a, b, (((1,), (1,)), ((), ())), preferred_element_type=F32)


def _split3(x):
    hi = x.astype(BF16)
    r = x - hi.astype(F32)
    mid = r.astype(BF16)
    lo = (r - mid.astype(F32)).astype(BF16)
    return hi, mid, lo


def _dot_01_lhs(m01, x):
    hi, mid, lo = _split3(x)
    return _dot(m01, hi) + _dot(m01, mid) + _dot(m01, lo)


def _dot_01_rhs(x, m01):
    hi, mid, lo = _split3(x)
    return _dot(hi, m01) + _dot(mid, m01) + _dot(lo, m01)


def _rms(x, g):
    return x * lax.rsqrt(jnp.mean(x * x, axis=-1, keepdims=True) + EPS) * g


def _gelu(x):
    return 0.5 * x * (1.0 + jnp.tanh(0.7978845608028654 * (x + 0.044715 * (x * x * x))))


def _sigmoid(x):
    return 1.0 / (1.0 + jnp.exp(-x))


def _silu(x):
    return x * _sigmoid(x)


def _softmax_masked(s, mask):
    sm = jnp.where(mask, s, NEG)
    m = jnp.max(sm, axis=-1, keepdims=True)
    e = jnp.where(mask, jnp.exp(sm - m), 0.0)
    return e / jnp.maximum(jnp.sum(e, axis=-1, keepdims=True), 1e-30)


def _in_proj_body(x_ref, g_ref, w_ref, *outs):
    xn = _rms(x_ref[...], g_ref[...]).astype(BF16)
    for (lo, hi), o_ref in zip(AB_SEGS, outs):
        o_ref[...] = _dot(xn, w_ref[:, lo:hi]).astype(o_ref.dtype)


def in_proj_ab(x, g, w):
    m = x.shape[0]
    tm = min(m, 512)
    widths = [hi - lo for lo, hi in AB_SEGS]
    dtypes = [F32, BF16, F32, F32, F32, F32]
    return pl.pallas_call(
        _in_proj_body,
        grid=(m // tm,),
        in_specs=[pl.BlockSpec((tm, D_MODEL), lambda i: (i, 0)),
                  _const_spec((1, D_MODEL)),
                  _const_spec((D_MODEL, AB_COLS_PAD))],
        out_specs=[pl.BlockSpec((tm, w_), lambda i: (i, 0)) for w_ in widths],
        out_shape=[jax.ShapeDtypeStruct((m, w_), dt) for w_, dt in zip(widths, dtypes)],
        compiler_params=_cparams(("parallel",), 56),
        name="in_proj_ab",
    )(x, g.reshape(1, D_MODEL), w)


def _in_proj_t_body(n_alias, x_ref, g_ref, w_ref, wkv_ref, *refs):
    za_ref, q_ref, gt_ref, cmp_ref, sel_ref, win_ref = refs[n_alias:]
    xn = _rms(x_ref[...], g_ref[...]).astype(BF16)
    za_ref[...] = _dot(xn, w_ref[:, 0:2048])
    q_ref[...] = _dot(xn, w_ref[:, 2048:2560]).astype(q_ref.dtype)
    gt_ref[...] = _dot(xn, w_ref[:, 2560:2688])
    kvt = _dot_nt(wkv_ref[...], xn)
    for i, o_ref in enumerate((cmp_ref, sel_ref, win_ref)):
        o_ref[...] = kvt[i * KV_COLS:(i + 1) * KV_COLS]


def in_proj_ab_t(x, g, w, wkv_t, j, n_ab, batch, prev):
    m = x.shape[0]
    t = m // batch
    tm = 512
    nt = t // tm
    n_alias = 0 if prev is None else 3
    row = lambda w_: pl.BlockSpec((tm, w_), lambda b, i: (b * nt + i, 0))
    kv_spec = pl.BlockSpec((None, None, KV_COLS, tm), lambda b, i: (j, b, 0, i))
    kv_shape = jax.ShapeDtypeStruct((n_ab, batch, KV_COLS, t), F32)
    return pl.pallas_call(
        functools.partial(_in_proj_t_body, n_alias),
        grid=(batch, nt),
        in_specs=[row(D_MODEL), _const_spec((1, D_MODEL)), _const_spec(w.shape), _const_spec(wkv_t.shape)]
        + [pl.BlockSpec(memory_space=pl.ANY)] * n_alias,
        out_specs=[row(2048), row(D_B), row(LANES), kv_spec, kv_spec, kv_spec],
        out_shape=[jax.ShapeDtypeStruct((m, 2048), F32), jax.ShapeDtypeStruct((m, D_B), BF16),
                   jax.ShapeDtypeStruct((m, LANES), F32), kv_shape, kv_shape, kv_shape],
        input_output_aliases={4 + i: 3 + i for i in range(n_alias)},
        compiler_params=_cparams(("parallel", "parallel"), 56),
        name="in_proj_ab_t",
    )(x, g.reshape(1, D_MODEL), w, wkv_t, *(() if prev is None else prev))


def _lower_bound(raw, j):
    m = jnp.max(raw, axis=0, keepdims=True)
    e = jnp.exp(raw - m)
    sm = e / jnp.sum(e, axis=0, keepdims=True)
    cs = sm[0:1]
    for i in range(1, j + 1):
        cs = cs + sm[i:i + 1]
    return cs - sm[0:1]


def _hgrn_consts(c):
    t = np.arange(c)[:, None]
    u = np.arange(c)[None, :]
    mats = [u <= t, u > t]
    masks = [t == u]
    for lvl in range(1, int(np.log2(c)) + 1):
        m = 1 << lvl
        half = m >> 1
        mid = (t // m) * m + half
        upper = (t % m) >= half
        mats.append((upper & (u >= mid) & (u <= t)) | (~upper & (u > t) & (u <= mid - 1)))
        masks.append(((t // m) == (u // m)) & upper & ((u % m) < half))
    return (np.concatenate(mats, 0).astype(np.float32), np.stack(masks).astype(np.float32))


def _hgrn_prompt_body(j, za_ref, lbraw_ref, hn_ref, mst_ref, lm_ref, o_ref, sfin_ref, st_ref):
    c = HGRN_CHUNK
    ci = pl.program_id(1)

    @pl.when(ci == 0)
    def _():
        st_ref[...] = jnp.zeros(st_ref.shape, F32)

    lb = _lower_bound(lbraw_ref[...], j)
    fg = lb + (1.0 - lb) * _sigmoid(za_ref[:, 512:1024])
    lf = jnp.log(fg)
    q = _silu(za_ref[:, 0:512])
    k = 1.0 - fg
    v = za_ref[:, 1024:1536]
    gg = _silu(za_ref[:, 1536:2048])
    mst = mst_ref[...]
    nlev = lm_ref.shape[0]
    for h in range(H_A):
        sl = slice(h * DK_A, (h + 1) * DK_A)
        qh, kh, vh = q[:, sl], k[:, sl], v[:, sl]
        x = jnp.exp(_dot_01_lhs(mst, lf[:, sl]))
        att = lm_ref[0] * _dot_nt(qh.astype(BF16), kh.astype(BF16))
        for lvl in range(1, nlev):
            d = x[(1 + lvl) * c:(2 + lvl) * c]
            att = att + lm_ref[lvl] * _dot_nt((qh * d).astype(BF16), (kh * d).astype(BF16))
        st = st_ref[h]
        o = _dot_nt((qh * x[0:c]).astype(BF16), st.astype(BF16)) + _dot(att.astype(BF16), vh.astype(BF16))
        kout = (kh * x[c:2 * c]).astype(BF16)
        st_new = st * x[c - 1:c] + _dot(vh.T.astype(BF16), kout)
        st_ref[h] = st_new
        on = _rms(o, hn_ref[...]) * gg[:, sl]
        o_ref[:, sl] = on.astype(o_ref.dtype)

        @pl.when(ci == pl.num_programs(1) - 1)
        def _():
            sfin_ref[h] = st_new.T


def hgrn_prompt(za, lb_raw, hnorm, j, batch):
    m = za.shape[0]
    t = m // batch
    c = HGRN_CHUNK
    nchunk = t // c
    mst, lm = _hgrn_consts(c)
    return pl.pallas_call(
        functools.partial(_hgrn_prompt_body, j),
        grid=(batch, nchunk),
        in_specs=[pl.BlockSpec((c, 2048), lambda b, i: (b * nchunk + i, 0)),
                  _const_spec(lb_raw.shape),
                  _const_spec((1, DK_A)),
                  _const_spec(mst.shape),
                  _const_spec(lm.shape)],
        out_specs=[pl.BlockSpec((c, D_A), lambda b, i: (b * nchunk + i, 0)),
                   pl.BlockSpec((None, H_A, DK_A, DK_A), lambda b, i: (b, 0, 0, 0))],
        out_shape=[jax.ShapeDtypeStruct((m, D_A), BF16),
                   jax.ShapeDtypeStruct((batch, H_A, DK_A, DK_A), F32)],
        scratch_shapes=[pltpu.VMEM((H_A, DK_A, DK_A), F32)],
        compiler_params=_cparams(("parallel", "arbitrary"), 48),
        name="hgrn_prompt",
    )(za, lb_raw, hnorm.reshape(1, DK_A), jnp.asarray(mst, BF16), jnp.asarray(lm, F32))


def _hgrn_step_body(j, bt, za_ref, s_ref, lbraw_ref, hn_ref, o_ref, snew_ref):
    lb = _lower_bound(lbraw_ref[...], j)
    fg = lb + (1.0 - lb) * _sigmoid(za_ref[:, 512:1024])
    q = _silu(za_ref[:, 0:512])
    k = 1.0 - fg
    v = za_ref[:, 1024:1536]
    gg = _silu(za_ref[:, 1536:2048])
    zpad = jnp.zeros((LANES - bt, DK_A), F32)

    def cols(a):
        return jnp.concatenate([a, zpad], axis=0).T

    for h in range(H_A):
        sl = slice(h * DK_A, (h + 1) * DK_A)
        qc, fc, kc = cols(q[:, sl]), cols(fg[:, sl]), cols(k[:, sl])
        for b in range(bt):
            s_new = fc[:, b:b + 1] * s_ref[b, h] + kc[:, b:b + 1] * v[b:b + 1, sl]
            snew_ref[b, h] = s_new
            o = jnp.sum(qc[:, b:b + 1] * s_new, axis=0, keepdims=True)
            o_ref[b:b + 1, sl] = (_rms(o, hn_ref[...]) * gg[b:b + 1, sl]).astype(o_ref.dtype)


def hgrn_step(za, state, lb_raw, hnorm, j):
    nb = za.shape[0]
    bt = 8
    return pl.pallas_call(
        functools.partial(_hgrn_step_body, j, bt),
        grid=(nb // bt,),
        in_specs=[pl.BlockSpec((bt, 2048), lambda i: (i, 0)),
                  pl.BlockSpec((None, bt, H_A, DK_A, DK_A), lambda i: (j, i, 0, 0, 0)),
                  _const_spec(lb_raw.shape),
                  _const_spec((1, DK_A))],
        out_specs=[pl.BlockSpec((bt, D_A), lambda i: (i, 0)),
                   pl.BlockSpec((bt, H_A, DK_A, DK_A), lambda i: (i, 0, 0, 0))],
        out_shape=[jax.ShapeDtypeStruct((nb, D_A), BF16),
                   jax.ShapeDtypeStruct(state.shape[1:], F32)],
        compiler_params=_cparams(("parallel",)),
        name="hgrn_step",
    )(za, state, lb_raw, hnorm.reshape(1, DK_A))


def _compress_weights(pe, w1, w2):
    eye = jnp.eye(N_KV_B, dtype=F32)
    w1r = w1.reshape(2, CMP_BLOCK, HD_B, CMP_HIDDEN)
    w1x = jnp.einsum("krdh,gG->rkgdGh", w1r, eye).reshape(CMP_BLOCK, 2, N_KV_B * HD_B, N_KV_B * CMP_HIDDEN)
    pex = jnp.broadcast_to(pe.transpose(1, 0, 2)[:, :, None, :], (CMP_BLOCK, 2, N_KV_B, HD_B))
    pex = pex.reshape(CMP_BLOCK, 2, 1, N_KV_B * HD_B)
    w2xt = jnp.einsum("khd,gG->kGdgh", w2, eye).reshape(2, N_KV_B * HD_B, N_KV_B * CMP_HIDDEN)
    return w1x.astype(BF16), pex.astype(F32), w2xt.astype(BF16)


def _compress_body(npp, nseg, *refs):
    pages = refs[1:1 + npp]
    w1_ref, pe_ref, w2_ref, o_ref, xk, xv, bbuf = refs[1 + npp:]
    s = pl.program_id(1)
    for i, pg in enumerate(pages):
        row0 = pl.multiple_of((s * npp + i) * PAGE_SIZE, PAGE_SIZE)
        xk[pl.ds(row0, PAGE_SIZE), :] = pg[0].T
        xv[pl.ds(row0, PAGE_SIZE), :] = pg[1].T

    @pl.when(s == pl.num_programs(1) - 1)
    def _():
        bbuf[pl.ds(nseg, 8), :] = jnp.zeros((8, N_KV_B * CMP_HIDDEN), F32)
        for kv, xbuf in enumerate((xk, xv)):
            acc_a = jnp.zeros((nseg, N_KV_B * CMP_HIDDEN), F32)
            acc_b = jnp.zeros((nseg, N_KV_B * CMP_HIDDEN), F32)
            for r in range(CMP_STRIDE):
                xr = xbuf[pl.ds(r, nseg, stride=CMP_STRIDE), :]
                acc_a = acc_a + _dot((xr + pe_ref[r, kv]).astype(BF16), w1_ref[r, kv])
                acc_b = acc_b + _dot((xr + pe_ref[CMP_STRIDE + r, kv]).astype(BF16), w1_ref[CMP_STRIDE + r, kv])
            bbuf[pl.ds(0, nseg), :] = acc_b
            hid = _gelu(acc_a + bbuf[pl.ds(1, nseg), :])
            o_ref[kv] = _dot_nt(w2_ref[kv], hid.astype(BF16))


def compress_pages(table, pages_t, j, lane_paged, w1x, pex, w2xt, npp):
    nb, npg = table.shape
    nseg = npg * PAGE_SIZE // CMP_STRIDE
    half = KV_COLS // 2
    if lane_paged:
        imap = lambda b, s, t, i: (j, b, 0, 0, t[jnp.minimum(b, nb - 1), s * npp + i])
    else:
        imap = lambda b, s, t, i: (j, t[jnp.minimum(b, nb - 1), s * npp + i], 0, 0, 0)
    page_specs = [pl.BlockSpec((None, None, 2, half, PAGE_SIZE), functools.partial(imap, i=i)) for i in range(npp)]
    grid_spec = pltpu.PrefetchScalarGridSpec(
        num_scalar_prefetch=1,
        grid=(nb, npg // npp),
        in_specs=page_specs + [_const_spec(w1x.shape), _const_spec(pex.shape), _const_spec(w2xt.shape)],
        out_specs=pl.BlockSpec((None, 2, half, nseg), lambda b, s, t: (b, 0, 0, 0)),
        scratch_shapes=[pltpu.VMEM((npg * PAGE_SIZE, half), F32),
                        pltpu.VMEM((npg * PAGE_SIZE, half), F32),
                        pltpu.VMEM((nseg + 8, N_KV_B * CMP_HIDDEN), F32)],
    )
    return pl.pallas_call(
        functools.partial(_compress_body, npp, nseg),
        grid_spec=grid_spec,
        out_shape=jax.ShapeDtypeStruct((nb, 2, half, nseg), F32),
        compiler_params=_cparams(("parallel", "arbitrary"), 56),
        name="compress_pages",
    )(table, *([pages_t] * npp), w1x, pex, w2xt)


def _overlap_matrix(n_cmp_pad, n_sel_pad, n_cmp, n_sel):
    cs = np.arange(n_cmp_pad)[:, None] * CMP_STRIDE
    ss = np.arange(n_sel_pad)[None, :] * SEL_BLOCK
    ov = (cs < ss + SEL_BLOCK) & (cs + CMP_BLOCK > ss)
    ov &= (np.arange(n_cmp_pad)[:, None] < n_cmp) & (np.arange(n_sel_pad)[None, :] < n_sel)
    return ov.astype(np.float32)


def _rank(imp, jidx, cur):
    valid = jidx <= cur
    dist = cur - jidx
    forced = (jidx == 0) | ((dist >= 0) & (dist < N_LOCAL))
    return jnp.where(valid, imp + BIG * forced.astype(F32), -BIG)


NSA_TQ = 256
NSA_TK = 512
NSA_TW = 256
NSA_RB = 64


def _nsa_prompt_body(n_cmp, n_sel, q_ref, kct_ref, vct_ref, kst_ref, vst_ref, kwt_ref, vwt_ref, gt_ref, ov_ref,
                     eb_ref, o_ref, kc2, vc2e, vc2o, k2e, k2o, v2e, v2o, kw2, vw2e, vw2o, s_sc, e_sc, m_sc, l_sc,
                     acc_sc):
    tq, tk, tw, rb_ = NSA_TQ, NSA_TK, NSA_TW, NSA_RB
    hd = HD_B
    g = pl.program_id(1)
    qi = pl.program_id(2)
    t0 = qi * tq
    t_len = kst_ref.shape[1]
    ncp = kct_ref.shape[1]

    @pl.when(qi == 0)
    def _():
        zc = jnp.zeros((hd, ncp), BF16)
        kct = kct_ref[...].astype(BF16)
        vct = vct_ref[...].astype(BF16)
        kc2[0:hd] = kct
        kc2[hd:2 * hd] = kct
        vc2e[0:hd] = vct
        vc2e[hd:2 * hd] = zc
        vc2o[0:hd] = zc
        vc2o[hd:2 * hd] = vct
        zk = jnp.zeros((hd, tk), BF16)
        for kt in range(t_len // tk):
            cols = slice(kt * tk, (kt + 1) * tk)
            kt_ = kst_ref[:, cols].astype(BF16)
            vt_ = vst_ref[:, cols].astype(BF16)
            k2e[kt, 0:hd] = kt_
            k2e[kt, hd:2 * hd] = eb_ref[kt]
            k2o[kt, 0:hd] = eb_ref[kt]
            k2o[kt, hd:2 * hd] = kt_
            v2e[kt, 0:hd] = vt_
            v2e[kt, hd:2 * hd] = zk
            v2o[kt, 0:hd] = zk
            v2o[kt, hd:2 * hd] = vt_
        zw = jnp.zeros((hd, tw), BF16)
        for wt in range(t_len // tw):
            cols = slice(wt * tw, (wt + 1) * tw)
            kt_ = kwt_ref[:, cols].astype(BF16)
            vt_ = vwt_ref[:, cols].astype(BF16)
            kw2[wt, 0:hd] = kt_
            kw2[wt, hd:2 * hd] = kt_
            vw2e[wt, 0:hd] = vt_
            vw2e[wt, hd:2 * hd] = zw
            vw2o[wt, 0:hd] = zw
            vw2o[wt, hd:2 * hd] = vt_

    lane = lax.broadcasted_iota(I32, (1, LANES), 1)
    even = lane < hd
    tpos = t0 + lax.broadcasted_iota(I32, (tq, 1), 0)
    gts = _sigmoid(gt_ref[...])
    gsel = jnp.where(g == 0, gts[:, 0:REP_B * 3], gts[:, REP_B * 3:2 * REP_B * 3])
    gate = lambda r, c: gsel[:, 3 * r + c:3 * r + c + 1]
    zq = jnp.zeros((tq, LANES), q_ref.dtype)
    q128 = [q_ref[:, p * LANES:(p + 1) * LANES] * ATT_SCALE for p in range(REP_B // 2)]
    qh = [jnp.where(even if r % 2 == 0 else ~even, q128[r // 2], zq) for r in range(REP_B)]

    n_idx = lax.broadcasted_iota(I32, (1, ncp), 1)
    cmask = ((n_idx * CMP_STRIDE + (CMP_BLOCK - 1)) <= tpos) & (n_idx < n_cmp)
    o_cmp = []
    psum = jnp.zeros((tq, ncp), F32)
    for r in range(REP_B):
        p = _softmax_masked(_dot(qh[r], kc2[...]), cmask)
        psum = psum + p
        o_cmp.append(_dot_nt(p.astype(BF16), (vc2e if r % 2 == 0 else vc2o)[...]))
    imp = _dot_01_rhs(psum, ov_ref[...])

    imp_t = imp.T[0:n_sel]
    jidx = lax.broadcasted_iota(I32, (n_sel, tq), 0)
    cur = jnp.right_shift(t0 + lax.broadcasted_iota(I32, (n_sel, tq), 1), 6)
    rank = _rank(imp_t, jidx, cur)
    cnt = jnp.zeros((n_sel, tq), I32)
    for i in range(n_sel):
        ri = rank[i:i + 1]
        beats = (ri > rank) | ((ri == rank) & (jidx > i))
        cnt = cnt + beats.astype(I32)
    drop_t = (cnt >= min(N_SEL, n_sel)).astype(F32)
    if n_sel < hd:
        drop_t = jnp.concatenate([drop_t, jnp.zeros((hd - n_sel, tq), F32)], axis=0)
    drop = jnp.concatenate([drop_t, drop_t], axis=0).T.astype(q_ref.dtype)
    q2 = [jnp.where(even if r % 2 == 0 else ~even, q128[r // 2], drop) for r in range(REP_B)]

    m_sc[...] = jnp.full(m_sc.shape, NEG_MASK, F32)
    l_sc[...] = jnp.zeros(l_sc.shape, F32)
    acc_sc[...] = jnp.zeros(acc_sc.shape, F32)

    def kv_tile(kt, causal):
        k0 = kt * tk
        kpos = k0 + lax.broadcasted_iota(I32, (1, tk), 1)
        for r in range(REP_B):
            s_sc[...] = _dot(q2[r], (k2e if r % 2 == 0 else k2o)[kt])
            for rb in range(tq // rb_):
                rows = slice(rb * rb_, (rb + 1) * rb_)
                sb = s_sc[rows]
                if causal:
                    sb = jnp.where(kpos <= tpos[rows], sb, NEG_MASK)
                m_old = m_sc[r, rows]
                m_new = jnp.maximum(m_old, jnp.max(sb, axis=-1, keepdims=True))
                alpha = jnp.exp(m_old - m_new)
                e = jnp.exp(sb - m_new)
                l_sc[r, rows] = alpha * l_sc[r, rows] + jnp.sum(e, axis=-1, keepdims=True)
                m_sc[r, rows] = m_new
                e_sc[rows] = e.astype(e_sc.dtype)
                acc_sc[r, rows] = alpha * acc_sc[r, rows]
            acc_sc[r] += _dot_nt(e_sc[...], (v2e if r % 2 == 0 else v2o)[kt])

    n_full = t0 // tk

    def full_tile(kt, carry):
        kv_tile(kt, False)
        return carry

    lax.fori_loop(0, n_full, full_tile, 0)
    kv_tile(n_full, True)
    o_sel = [acc_sc[r] / jnp.maximum(l_sc[r], 1e-30) for r in range(REP_B)]

    nwt = (WINDOW + tq) // tw
    w0 = jnp.maximum(t0 - WINDOW, 0)
    wt0 = w0 // tw
    dist = tpos - (w0 + lax.broadcasted_iota(I32, (1, nwt * tw), 1))
    wmask = (dist >= 0) & (dist < WINDOW)
    o_win = []
    for r in range(REP_B):
        sw = jnp.concatenate([_dot(qh[r], kw2[wt0 + c]) for c in range(nwt)], axis=1)
        pw = _softmax_masked(sw, wmask).astype(BF16)
        vw = vw2e if r % 2 == 0 else vw2o
        ow = _dot_nt(pw[:, 0:tw], vw[wt0])
        for c in range(1, nwt):
            ow = ow + _dot_nt(pw[:, c * tw:(c + 1) * tw], vw[wt0 + c])
        o_win.append(ow)

    for p in range(REP_B // 2):
        acc = jnp.zeros((tq, LANES), F32)
        for r in (2 * p, 2 * p + 1):
            acc = acc + gate(r, 0) * o_cmp[r] + gate(r, 1) * o_sel[r] + gate(r, 2) * o_win[r]
        o_ref[:, p * LANES:(p + 1) * LANES] = acc.astype(o_ref.dtype)


def nsa_prompt(q, kct, kvt_sel, kvt_win, gates, j, batch, n_cmp):
    m = q.shape[0]
    t = m // batch
    tq, tk, tw = NSA_TQ, NSA_TK, NSA_TW
    hd = HD_B
    n_sel = t // SEL_BLOCK
    ncp = kct.shape[3]
    nqt = t // tq
    ov = jnp.asarray(_overlap_matrix(ncp, LANES, n_cmp, n_sel), BF16)
    kb = np.arange(t) // SEL_BLOCK
    eb = np.where(np.arange(hd)[:, None] == kb[None, :], NEG_MASK, 0.0).astype(np.float32)
    eb = jnp.asarray(eb.reshape(hd, t // tk, tk).transpose(1, 0, 2), BF16)
    n_ab = kvt_sel.shape[0]
    kct4 = kct.reshape(batch, 2 * N_KV_B, hd, ncp)
    sel5 = kvt_sel.reshape(n_ab, batch, 2 * N_KV_B, hd, t)
    win5 = kvt_win.reshape(n_ab, batch, 2 * N_KV_B, hd, t)
    cspec = lambda kv: pl.BlockSpec((None, None, hd, ncp), lambda b, g, i: (b, kv * N_KV_B + g, 0, 0))
    rspec = lambda kv: pl.BlockSpec((None, None, None, hd, t), lambda b, g, i: (j, b, kv * N_KV_B + g, 0, 0))
    slab = lambda n, w: pltpu.VMEM((n, 2 * hd, w), BF16)
    return pl.pallas_call(
        functools.partial(_nsa_prompt_body, n_cmp, n_sel),
        grid=(batch, N_KV_B, nqt),
        in_specs=[pl.BlockSpec((tq, REP_B * hd), lambda b, g, i: (b * nqt + i, g)),
                  cspec(0), cspec(1), rspec(0), rspec(1), rspec(0), rspec(1),
                  pl.BlockSpec((tq, LANES), lambda b, g, i: (b * nqt + i, 0)),
                  _const_spec(ov.shape), _const_spec(eb.shape)],
        out_specs=pl.BlockSpec((tq, REP_B * hd), lambda b, g, i: (b * nqt + i, g)),
        out_shape=jax.ShapeDtypeStruct((m, N_KV_B * REP_B * hd), BF16),
        scratch_shapes=[pltpu.VMEM((2 * hd, ncp), BF16), pltpu.VMEM((2 * hd, ncp), BF16), pltpu.VMEM((2 * hd, ncp), BF16),
                        slab(t // tk, tk), slab(t // tk, tk), slab(t // tk, tk), slab(t // tk, tk),
                        slab(t // tw, tw), slab(t // tw, tw), slab(t // tw, tw),
                        pltpu.VMEM((tq, tk), F32), pltpu.VMEM((tq, tk), BF16),
                        pltpu.VMEM((REP_B, tq, 1), F32), pltpu.VMEM((REP_B, tq, 1), F32),
                        pltpu.VMEM((REP_B, tq, LANES), F32)],
        compiler_params=_cparams(("parallel", "parallel", "arbitrary"), 56),
        name="nsa_prompt",
    )(q, kct4, kct4, sel5, sel5, win5, win5, gates, ov, eb)


def _group_queries(q8, g):
    z = jnp.zeros_like(q8)
    return jnp.concatenate([q8, z] if g == 0 else [z, q8], axis=1)


def _nsa_step_select_body(q_pos, n_cmp, n_sel, q_ref, kct_ref, ov_ref, ocmp_ref, idx_ref):
    q8 = q_ref[...] * ATT_SCALE
    ncp = kct_ref.shape[2]
    nsp = ov_ref.shape[1]
    n_idx = lax.broadcasted_iota(I32, (1, ncp), 1)
    cmask = ((n_idx * CMP_STRIDE + (CMP_BLOCK - 1)) <= q_pos) & (n_idx < n_cmp)
    head = lax.broadcasted_iota(I32, (N_KV_B * REP_B, 1), 0)
    kct = kct_ref[0].astype(BF16)
    vct = kct_ref[1].astype(BF16)
    o_all = jnp.zeros((N_KV_B * REP_B, HD_B), F32)
    for g in range(N_KV_B):
        ingrp = (head // REP_B) == g
        p = _softmax_masked(_dot(_group_queries(q8, g), kct), cmask)
        o_g = _dot_nt(p.astype(BF16), vct)
        o_all = jnp.where(ingrp, o_g[:, g * HD_B:(g + 1) * HD_B], o_all)
        imp8 = _dot_01_rhs(jnp.where(ingrp, p, 0.0), ov_ref[...])
        imp = jnp.sum(imp8, axis=0, keepdims=True)
        lane_j = lax.broadcasted_iota(I32, (nsp, nsp), 1)
        sub_i = lax.broadcasted_iota(I32, (nsp, nsp), 0)
        cur = q_pos // SEL_BLOCK
        rank_row = jnp.broadcast_to(_rank(imp, lane_j[0:1], cur), (nsp, nsp))
        rank_col = rank_row.T
        before = (rank_row > rank_col) | ((rank_row == rank_col) & (lane_j < sub_i))
        order = jnp.sum(before.astype(F32), axis=1, keepdims=True)
        slot = lax.broadcasted_iota(I32, (nsp, LANES), 1).astype(F32)
        hit = (order == slot) & (sub_i[:, 0:1] < n_sel)
        blk = lax.broadcasted_iota(I32, (nsp, LANES), 0).astype(F32)
        idx = jnp.sum(jnp.where(hit, blk, 0.0), axis=0, keepdims=True)
        idx_ref[g:g + 1, :] = idx.astype(I32)
    ocmp_ref[...] = o_all


def nsa_step_select(q, kct, n_cmp, n_sel, q_pos):
    nb = q.shape[0]
    ncp = kct.shape[3]
    nsp = 256
    nh = N_KV_B * REP_B
    ov = jnp.asarray(_overlap_matrix(ncp, nsp, n_cmp, n_sel), BF16)
    return pl.pallas_call(
        functools.partial(_nsa_step_select_body, q_pos, n_cmp, n_sel),
        grid=(nb,),
        in_specs=[pl.BlockSpec((None, nh, HD_B), lambda b: (b, 0, 0)),
                  pl.BlockSpec((None, 2, KV_COLS // 2, ncp), lambda b: (b, 0, 0, 0)),
                  _const_spec(ov.shape)],
        out_specs=[pl.BlockSpec((None, nh, HD_B), lambda b: (b, 0, 0)),
                   pl.BlockSpec((None, N_KV_B, LANES), lambda b: (b, 0, 0))],
        out_shape=[jax.ShapeDtypeStruct((nb, nh, HD_B), F32),
                   jax.ShapeDtypeStruct((nb, N_KV_B, LANES), I32)],
        compiler_params=_cparams(("parallel",)),
        name="nsa_step_select",
    )(q, kct, ov)


def _attend_with_new(qp, kt, vt, kmask, k_new, v_new, new_ok):
    s = _dot(qp, kt)
    s_new = jnp.sum(qp.astype(F32) * k_new, axis=1, keepdims=True)
    sm = jnp.where(kmask, s, NEG)
    sn = jnp.where(new_ok, s_new, NEG)
    m = jnp.maximum(jnp.max(sm, axis=-1, keepdims=True), sn)
    e = jnp.where(kmask, jnp.exp(sm - m), 0.0)
    e_new = jnp.where(new_ok, jnp.exp(sn - m), 0.0)
    l = jnp.sum(e, axis=-1, keepdims=True) + e_new
    return (_dot_nt(e.astype(BF16), vt) + e_new * v_new) / jnp.maximum(l, 1e-30)


def _nsa_step_attend_body(q_pos, past_len, k_top, *refs):
    tbl_ref, idx_ref = refs[0], refs[1]
    pages = refs[2:2 + N_KV_B * k_top]
    (q_ref, ksn_ref, win_ref, kwn_ref, ocmp_ref, gt_ref, o_ref, kt_sc, vt_sc) = refs[2 + N_KV_B * k_top:]
    b = pl.program_id(0)
    nh = N_KV_B * REP_B
    half = KV_COLS // 2
    q8 = q_ref[...] * ATT_SCALE
    head = lax.broadcasted_iota(I32, (nh, 1), 0)
    nkeys = k_top * PAGE_SIZE
    wlen = win_ref.shape[2]
    wrow = lax.broadcasted_iota(I32, (1, wlen), 1)
    wdist = q_pos - (past_len - wlen + wrow)
    wmask = (wdist >= 0) & (wdist < WINDOW)
    kwt = win_ref[0].astype(BF16)
    vwt = win_ref[1].astype(BF16)
    ks_new, vs_new = ksn_ref[:, 0:half], ksn_ref[:, half:KV_COLS]
    kw_new, vw_new = kwn_ref[:, 0:half], kwn_ref[:, half:KV_COLS]
    o_sel = jnp.zeros((nh, HD_B), F32)
    o_win = jnp.zeros((nh, HD_B), F32)
    for g in range(N_KV_B):
        ingrp = (head // REP_B) == g
        qp = _group_queries(q8, g)
        has_new = jnp.zeros((), I32)
        krow = lax.broadcasted_iota(I32, (1, nkeys), 1)
        kmask = jnp.zeros((1, nkeys), jnp.bool_)
        for i in range(k_top):
            pg = pages[g * k_top + i]
            kt_sc[:, i * PAGE_SIZE:(i + 1) * PAGE_SIZE] = pg[0].astype(BF16)
            vt_sc[:, i * PAGE_SIZE:(i + 1) * PAGE_SIZE] = pg[1].astype(BF16)
            jb = idx_ref[b, g, i]
            kpos = (jb * SEL_BLOCK // PAGE_SIZE) * PAGE_SIZE + (krow - i * PAGE_SIZE)
            inblk = (krow >= i * PAGE_SIZE) & (krow < (i + 1) * PAGE_SIZE) & (kpos // SEL_BLOCK == jb)
            kmask = kmask | (inblk & (kpos <= q_pos) & (kpos < past_len))
            has_new = has_new | (jb * SEL_BLOCK + SEL_BLOCK > past_len).astype(I32)
        os_ = _attend_with_new(qp, kt_sc[...], vt_sc[...], kmask, ks_new, vs_new, has_new > 0)
        o_sel = jnp.where(ingrp, os_[:, g * HD_B:(g + 1) * HD_B], o_sel)
        ow_ = _attend_with_new(qp, kwt, vwt, wmask, kw_new, vw_new, True)
        o_win = jnp.where(ingrp, ow_[:, g * HD_B:(g + 1) * HD_B], o_win)
    gt = _sigmoid(gt_ref[...])
    o_ref[...] = (gt[:, 0:1] * ocmp_ref[...] + gt[:, 1:2] * o_sel + gt[:, 2:3] * o_win).astype(o_ref.dtype)


def nsa_step_attend(table, idx, cache_sel_t, win_t, j, q, kvs_new, kvw_new, o_cmp, gates, q_pos, past_len):
    nb = q.shape[0]
    k_top = N_SEL
    wlen = win_t.shape[4]
    half = KV_COLS // 2
    per_page = PAGE_SIZE // SEL_BLOCK
    last_cached = past_len // SEL_BLOCK - 1

    def page_map(b, t, ix, g, i):
        bb = jnp.minimum(b, nb - 1)
        jb = jnp.minimum(ix[bb, g, i], last_cached)
        return (j, t[bb, jb // per_page], 0, 0, 0)

    page_specs = [pl.BlockSpec((None, None, 2, half, PAGE_SIZE), functools.partial(page_map, g=g, i=i))
                  for g in range(N_KV_B) for i in range(k_top)]
    nh = N_KV_B * REP_B
    row_spec = lambda w: pl.BlockSpec((None, nh, w), lambda b, t, ix: (b, 0, 0))
    new_spec = pl.BlockSpec((None, 1, KV_COLS), lambda b, t, ix: (b, 0, 0))
    grid_spec = pltpu.PrefetchScalarGridSpec(
        num_scalar_prefetch=2,
        grid=(nb,),
        in_specs=page_specs + [row_spec(HD_B), new_spec,
                               pl.BlockSpec((None, None, 2, half, wlen), lambda b, t, ix: (j, b, 0, 0, 0)),
                               new_spec, row_spec(HD_B), row_spec(3)],
        out_specs=row_spec(HD_B),
        scratch_shapes=[pltpu.VMEM((half, k_top * PAGE_SIZE), BF16),
                        pltpu.VMEM((half, k_top * PAGE_SIZE), BF16)],
    )
    return pl.pallas_call(
        functools.partial(_nsa_step_attend_body, q_pos, past_len, k_top),
        grid_spec=grid_spec,
        out_shape=jax.ShapeDtypeStruct((nb, nh, HD_B), BF16),
        compiler_params=_cparams(("arbitrary",)),
        name="nsa_step_attend",
    )(table, idx, *([cache_sel_t] * (N_KV_B * k_top)), q, kvs_new, win_t, kvw_new, o_cmp, gates)


def _out_proj_body(oa_ref, ob_ref, w_ref, r_ref, o_ref):
    o_ref[...] = r_ref[...] + _dot(oa_ref[...], w_ref[0:D_A, :]) + _dot(ob_ref[...], w_ref[D_A:D_A + D_B, :])


def out_proj(o_a, o_b, w, resid):
    m = resid.shape[0]
    tm = min(m, 512)
    return pl.pallas_call(
        _out_proj_body,
        grid=(m // tm,),
        in_specs=[pl.BlockSpec((tm, D_A), lambda i: (i, 0)), pl.BlockSpec((tm, D_B), lambda i: (i, 0)),
                  _const_spec(w.shape), pl.BlockSpec((tm, D_MODEL), lambda i: (i, 0))],
        out_specs=pl.BlockSpec((tm, D_MODEL), lambda i: (i, 0)),
        out_shape=jax.ShapeDtypeStruct((m, D_MODEL), F32),
        compiler_params=_cparams(("parallel",)),
        name="out_proj",
    )(o_a, o_b, w, resid)


def _ffn_body(final, x_ref, g_ref, w1_ref, w2_ref, gf_ref, o_ref, xn_sc, acc_sc):
    jf = pl.program_id(1)

    @pl.when(jf == 0)
    def _():
        xn_sc[...] = _rms(x_ref[...], g_ref[...]).astype(BF16)
        acc_sc[...] = jnp.zeros(acc_sc.shape, F32)

    h = jnp.maximum(_dot(xn_sc[...], w1_ref[...]), 0.0)
    acc_sc[...] += _dot((h * h).astype(BF16), w2_ref[...])

    @pl.when(jf == pl.num_programs(1) - 1)
    def _():
        y = x_ref[...] + acc_sc[...]
        o_ref[...] = _rms(y, gf_ref[...]) if final else y


def ffn(x, g, w1, w2, g_final, final):
    m = x.shape[0]
    tm = min(m, 1024)
    tf = 512
    return pl.pallas_call(
        functools.partial(_ffn_body, final),
        grid=(m // tm, D_FF // tf),
        in_specs=[pl.BlockSpec((tm, D_MODEL), lambda i, j: (i, 0)),
                  _const_spec((1, D_MODEL)),
                  pl.BlockSpec((D_MODEL, tf), lambda i, j: (0, j)),
                  pl.BlockSpec((tf, D_MODEL), lambda i, j: (j, 0)),
                  _const_spec((1, D_MODEL))],
        out_specs=pl.BlockSpec((tm, D_MODEL), lambda i, j: (i, 0)),
        out_shape=jax.ShapeDtypeStruct((m, D_MODEL), F32),
        scratch_shapes=[pltpu.VMEM((tm, D_MODEL), BF16), pltpu.VMEM((tm, D_MODEL), F32)],
        compiler_params=_cparams(("parallel", "arbitrary"), 48),
        name="ffn",
    )(x, g.reshape(1, D_MODEL), w1, w2, g_final.reshape(1, D_MODEL))


def _layernorm(v, g, b):
    mu = jnp.mean(v, axis=-1, keepdims=True)
    var = jnp.mean(jnp.square(v - mu), axis=-1, keepdims=True)
    return (v - mu) * lax.rsqrt(var + EPS) * g + b


def _gmlp_prompt_body(tm, x_ref, g_ref, win_ref, lng_ref, lnb_ref, ws_ref, bs_ref, wout_ref, o_ref, mix_sc):
    x = x_ref[...]
    z = _gelu(_dot(_rms(x, g_ref[...]).astype(BF16), win_ref[...]))
    u = z[:, 0:D_MODEL]
    v = _layernorm(z[:, D_MODEL:2 * D_MODEL], lng_ref[...], lnb_ref[...]).astype(BF16)
    row = lax.broadcasted_iota(I32, (CHUNK_C, CHUNK_C), 0)
    col = lax.broadcasted_iota(I32, (CHUNK_C, CHUNK_C), 1)
    gcd = D_MODEL // G_C
    for gi in range(G_C):
        ws = jnp.where(row >= col, ws_ref[gi], 0.0).astype(BF16)
        for ch in range(tm // CHUNK_C):
            rows = slice(ch * CHUNK_C, (ch + 1) * CHUNK_C)
            cols = slice(gi * gcd, (gi + 1) * gcd)
            mix_sc[rows, cols] = _dot(ws, v[rows, cols]) + bs_ref[gi]
    o_ref[...] = x + _dot((u * mix_sc[...]).astype(BF16), wout_ref[...])


def gmlp_prompt(x, g, w_in, ln_g, ln_b, w_s, b_s, w_out):
    m = x.shape[0]
    tm = 256
    return pl.pallas_call(
        functools.partial(_gmlp_prompt_body, tm),
        grid=(m // tm,),
        in_specs=[pl.BlockSpec((tm, D_MODEL), lambda i: (i, 0)),
                  _const_spec((1, D_MODEL)), _const_spec(w_in.shape),
                  _const_spec((1, D_MODEL)), _const_spec((1, D_MODEL)),
                  _const_spec(w_s.shape), _const_spec((G_C, CHUNK_C, 1)), _const_spec(w_out.shape)],
        out_specs=pl.BlockSpec((tm, D_MODEL), lambda i: (i, 0)),
        out_shape=jax.ShapeDtypeStruct((m, D_MODEL), F32),
        scratch_shapes=[pltpu.VMEM((tm, D_MODEL), F32)],
        compiler_params=_cparams(("parallel",), 48),
        name="gmlp_prompt",
    )(x, g.reshape(1, D_MODEL), w_in, ln_g.reshape(1, D_MODEL), ln_b.reshape(1, D_MODEL),
      w_s, b_s.reshape(G_C, CHUNK_C, 1), w_out)


def _gmlp_step_body(x_ref, g_ref, win_ref, lng_ref, lnb_ref, sc_ref, bi_ref, wout_ref, o_ref, v_ref):
    x = x_ref[...]
    z = _gelu(_dot(_rms(x, g_ref[...]).astype(BF16), win_ref[...]))
    v = _layernorm(z[:, D_MODEL:2 * D_MODEL], lng_ref[...], lnb_ref[...])
    v_ref[...] = v
    mix = v * sc_ref[...] + bi_ref[...]
    o_ref[...] = x + _dot((z[:, 0:D_MODEL] * mix).astype(BF16), wout_ref[...])


def gmlp_step(x, g, w_in, ln_g, ln_b, w_s, b_s, w_out):
    m = x.shape[0]
    gcd = D_MODEL // G_C
    scale = jnp.repeat(w_s[:, 0, 0], gcd).reshape(1, D_MODEL)
    bias = jnp.repeat(b_s[:, 0], gcd).reshape(1, D_MODEL)
    row = _const_spec((1, D_MODEL))
    full = _const_spec((m, D_MODEL))
    return pl.pallas_call(
        _gmlp_step_body,
        grid=(1,),
        in_specs=[full, row, _const_spec(w_in.shape), row, row, row, row, _const_spec(w_out.shape)],
        out_specs=[full, full],
        out_shape=[jax.ShapeDtypeStruct((m, D_MODEL), F32), jax.ShapeDtypeStruct((m, D_MODEL), F32)],
        compiler_params=_cparams(("arbitrary",), 48),
        name="gmlp_step",
    )(x, g.reshape(1, D_MODEL), w_in, ln_g.reshape(1, D_MODEL), ln_b.reshape(1, D_MODEL), scale, bias, w_out)


def _rows_last(a):
    nd = a.ndim
    perm = tuple(range(nd - 4)) + (nd - 3, nd - 2, nd - 1, nd - 4)
    a = a.transpose(perm)
    return a.reshape(a.shape[:nd - 3] + (N_KV_B * HD_B, a.shape[-1]))


def kernel(x_prompt, x_sample, cache_cmp_kv, cache_sel_kv, state_win_kv, state_hgrn, page_table, norm_mix, norm_ffn, norm_final, w_in_ab, w_out_ab, hgrn_lower_bounds, hgrn_norm, cmp_pe, cmp_w1, cmp_w2, w_in_c, ln_c_g, ln_c_b, w_s, b_s, w_out_c, w_ffn1, w_ffn2):
    bp, seq, d = x_prompt.shape
    bs = x_sample.shape[0]
    depth = norm_mix.shape[0]
    n_ab = w_in_ab.shape[0]
    npg = page_table.shape[1]
    past_len = npg * PAGE_SIZE
    hp = x_prompt.reshape(bp * seq, d)
    hs = x_sample.reshape(bs, d)

    prompt_pages = jnp.broadcast_to(jnp.arange(seq // PAGE_SIZE, dtype=I32), (bp, seq // PAGE_SIZE))
    n_cmp_p = seq // CMP_STRIDE - 1
    n_cmp_s = (past_len + 1) // CMP_STRIDE - 1
    n_sel_s = -(-(past_len + 1) // SEL_BLOCK)
    cache_cmp_t = _rows_last(cache_cmp_kv)
    cache_sel_t = _rows_last(cache_sel_kv)
    win_t = _rows_last(state_win_kv)
    kv_ofs = AB_SEGS[2][0]

    kvt_p = None
    cmp_s, sel_s, win_s, hg_p, hg_s, cv_s = [], [], [], [], [], []
    for layer in range(depth):
        j = layer // 2
        if layer % 2 == 0:
            w_full = w_in_ab[j]
            w_in = jnp.pad(w_full, ((0, 0), (0, AB_COLS_PAD - AB_COLS))).astype(BF16)
            w_zqg = jnp.concatenate([w_in[:, :kv_ofs], w_in[:, kv_ofs + 3 * KV_COLS:]], axis=1)
            w_kv_t = w_full[:, kv_ofs:kv_ofs + 3 * KV_COLS].T.astype(BF16)
            w_out = w_out_ab[j].astype(BF16)
            w1x, pex, w2xt = _compress_weights(cmp_pe[j], cmp_w1[j], cmp_w2[j])

            za, q, gt, kc_t, ks_t, kw_t = in_proj_ab_t(hp, norm_mix[layer], w_zqg, w_kv_t, j, n_ab, bp, kvt_p)
            kvt_p = (kc_t, ks_t, kw_t)
            o_a, s_fin = hgrn_prompt(za, hgrn_lower_bounds, hgrn_norm[j], j, bp)
            kct = compress_pages(prompt_pages, kc_t.reshape(n_ab, bp, 2, KV_COLS // 2, seq), j, True,
                                 w1x, pex, w2xt, npp=min(16, seq // PAGE_SIZE))
            o_b = nsa_prompt(q, kct, ks_t, kw_t, gt, j, bp, n_cmp_p)
            hp = out_proj(o_a, o_b, w_out, hp)
            hg_p.append(s_fin)

            za, q, kvc, kvs, kvw, gt = in_proj_ab(hs, norm_mix[layer], w_in)
            o_a, s_new = hgrn_step(za, state_hgrn, hgrn_lower_bounds, hgrn_norm[j], j)
            kct = compress_pages(page_table, cache_cmp_t, j, False, w1x, pex, w2xt, npp=min(16, npg))
            q8 = q.reshape(bs, N_KV_B * REP_B, HD_B)
            o_cmp, idx = nsa_step_select(q8, kct, n_cmp_s, n_sel_s, past_len)
            o_b = nsa_step_attend(page_table, idx, cache_sel_t, win_t, j, q8, kvs.reshape(bs, 1, KV_COLS),
                                  kvw.reshape(bs, 1, KV_COLS), o_cmp,
                                  gt[:, :3 * N_KV_B * REP_B].reshape(bs, N_KV_B * REP_B, 3), past_len, past_len)
            hs = out_proj(o_a, o_b.reshape(bs, D_B), w_out, hs)
            kv6s = lambda a: a.reshape(bs, 1, 2, N_KV_B, HD_B)
            cmp_s.append(kv6s(kvc)); sel_s.append(kv6s(kvs)); hg_s.append(s_new)
            win_s.append(jnp.concatenate([state_win_kv[j], kv6s(kvw)], axis=1)[:, 1:])
        else:
            w_in = w_in_c[j].astype(BF16)
            w_out = w_out_c[j].astype(BF16)
            hp = gmlp_prompt(hp, norm_mix[layer], w_in, ln_c_g[j], ln_c_b[j], w_s[j], b_s[j], w_out)
            hs, v_new = gmlp_step(hs, norm_mix[layer], w_in, ln_c_g[j], ln_c_b[j], w_s[j], b_s[j], w_out)
            cv_s.append(v_new.reshape(bs, 1, d))
        w1 = w_ffn1[layer].astype(BF16)
        w2 = w_ffn2[layer].astype(BF16)
        final = layer == depth - 1
        hp = ffn(hp, norm_ffn[layer], w1, w2, norm_final, final)
        hs = ffn(hs, norm_ffn[layer], w1, w2, norm_final, final)

    rows_first = lambda a: a.reshape(n_ab, bp, 2, N_KV_B, HD_B, seq).transpose(0, 1, 5, 2, 3, 4)
    cmp_p, sel_p, win_p = (rows_first(a) for a in kvt_p)
    return (hp.reshape(bp, seq, d), hs.reshape(bs, 1, d), cmp_p, jnp.stack(cmp_s),
            sel_p, jnp.stack(sel_s), win_p[:, :, seq - min(WINDOW, seq):], jnp.stack(win_s),
            jnp.stack(hg_p), jnp.stack(hg_s), jnp.stack(cv_s))
```

```python
import functools

import numpy as np
import jax
import jax.numpy as jnp
from jax import lax
from jax.experimental import pallas as pl
from jax.experimental.pallas import tpu as pltpu

F32 = jnp.float32
BF16 = jnp.bfloat16
I32 = jnp.int32

D_MODEL = 1024
H_A = 4
DK_A = 128
D_A = 512
D_B = 512
HD_B = 64
N_KV_B = 2
REP_B = 4
CMP_BLOCK = 32
CMP_STRIDE = 16
CMP_HIDDEN = 128
SEL_BLOCK = 64
N_SEL = 16
N_LOCAL = 2
WINDOW = 512
PAGE_SIZE = 128
CHUNK_C = 128
G_C = 8
D_FF = 4096
KV_COLS = 2 * N_KV_B * HD_B
ATT_SCALE = HD_B ** -0.5
EPS = 1e-6
BIG = 1e6
NEG = -1e30
NEG_MASK = -3e38

AB_SEGS = ((0, 2048), (2048, 2560), (2560, 2816), (2816, 3072), (3072, 3328), (3328, 3456))
AB_COLS = 3352
AB_COLS_PAD = 3456

HGRN_CHUNK = 128
LANES = 128


def _cparams(sem, vmem_mb=None):
    kw = dict(dimension_semantics=sem)
    if vmem_mb is not None:
        kw["vmem_limit_bytes"] = vmem_mb << 20
    return pltpu.CompilerParams(**kw)


def _const_spec(shape):
    return pl.BlockSpec(shape, lambda *a: (0,) * len(shape))


def _dot(a, b):
    return jnp.dot(a, b, preferred_element_type=F32)


def _dot_nt(a, b):
    return lax.dot_general(a, b, (((1,), (1,)), ((), ())), preferred_element_type=F32)


def _split3(x):
    hi = x.astype(BF16)
    r = x - hi.astype(F32)
    mid = r.astype(BF16)
    lo = (r - mid.astype(F32)).astype(BF16)
    return hi, mid, lo


def _dot_01_lhs(m01, x):
    hi, mid, lo = _split3(x)
    return _dot(m01, hi) + _dot(m01, mid) + _dot(m01, lo)


def _dot_01_rhs(x, m01):
    hi, mid, lo = _split3(x)
    return _dot(hi, m01) + _dot(mid, m01) + _dot(lo, m01)


def _rms(x, g):
    return x * lax.rsqrt(jnp.mean(x * x, axis=-1, keepdims=True) + EPS) * g


def _gelu(x):
    return 0.5 * x * (1.0 + jnp.tanh(0.7978845608028654 * (x + 0.044715 * (x * x * x))))


def _sigmoid(x):
    return 1.0 / (1.0 + jnp.exp(-x))


def _silu(x):
    return x * _sigmoid(x)


def _softmax_masked(s, mask):
    sm = jnp.where(mask, s, NEG)
    m = jnp.max(sm, axis=-1, keepdims=True)
    e = jnp.where(mask, jnp.exp(sm - m), 0.0)
    return e / jnp.maximum(jnp.sum(e, axis=-1, keepdims=True), 1e-30)


def _in_proj_body(x_ref, g_ref, w_ref, *outs):
    xn = _rms(x_ref[...], g_ref[...]).astype(BF16)
    for (lo, hi), o_ref in zip(AB_SEGS, outs):
        o_ref[...] = _dot(xn, w_ref[:, lo:hi]).astype(o_ref.dtype)


def in_proj_ab(x, g, w):
    m = x.shape[0]
    tm = min(m, 512)
    widths = [hi - lo for lo, hi in AB_SEGS]
    dtypes = [F32, BF16, F32, F32, F32, F32]
    return pl.pallas_call(
        _in_proj_body,
        grid=(m // tm,),
        in_specs=[pl.BlockSpec((tm, D_MODEL), lambda i: (i, 0)),
                  _const_spec((1, D_MODEL)),
                  _const_spec((D_MODEL, AB_COLS_PAD))],
        out_specs=[pl.BlockSpec((tm, w_), lambda i: (i, 0)) for w_ in widths],
        out_shape=[jax.ShapeDtypeStruct((m, w_), dt) for w_, dt in zip(widths, dtypes)],
        compiler_params=_cparams(("parallel",), 56),
        name="in_proj_ab",
    )(x, g.reshape(1, D_MODEL), w)


def _in_proj_t_body(n_alias, x_ref, g_ref, w_ref, wkv_ref, *refs):
    za_ref, q_ref, gt_ref, cmp_ref, sel_ref, win_ref = refs[n_alias:]
    xn = _rms(x_ref[...], g_ref[...]).astype(BF16)
    za_ref[...] = _dot(xn, w_ref[:, 0:2048])
    q_ref[...] = _dot(xn, w_ref[:, 2048:2560]).astype(q_ref.dtype)
    gt_ref[...] = _dot(xn, w_ref[:, 2560:2688])
    kvt = _dot_nt(wkv_ref[...], xn)
    for i, o_ref in enumerate((cmp_ref, sel_ref, win_ref)):
        o_ref[...] = kvt[i * KV_COLS:(i + 1) * KV_COLS]


def in_proj_ab_t(x, g, w, wkv_t, j, n_ab, batch, prev):
    m = x.shape[0]
    t = m // batch
    tm = 512
    nt = t // tm
    n_alias = 0 if prev is None else 3
    row = lambda w_: pl.BlockSpec((tm, w_), lambda b, i: (b * nt + i, 0))
    kv_spec = pl.BlockSpec((None, None, KV_COLS, tm), lambda b, i: (j, b, 0, i))
    kv_shape = jax.ShapeDtypeStruct((n_ab, batch, KV_COLS, t), F32)
    return pl.pallas_call(
        functools.partial(_in_proj_t_body, n_alias),
        grid=(batch, nt),
        in_specs=[row(D_MODEL), _const_spec((1, D_MODEL)), _const_spec(w.shape), _const_spec(wkv_t.shape)]
        + [pl.BlockSpec(memory_space=pl.ANY)] * n_alias,
        out_specs=[row(2048), row(D_B), row(LANES), kv_spec, kv_spec, kv_spec],
        out_shape=[jax.ShapeDtypeStruct((m, 2048), F32), jax.ShapeDtypeStruct((m, D_B), BF16),
                   jax.ShapeDtypeStruct((m, LANES), F32), kv_shape, kv_shape, kv_shape],
        input_output_aliases={4 + i: 3 + i for i in range(n_alias)},
        compiler_params=_cparams(("parallel", "parallel"), 56),
        name="in_proj_ab_t",
    )(x, g.reshape(1, D_MODEL), w, wkv_t, *(() if prev is None else prev))


def _lower_bound(raw, j):
    m = jnp.max(raw, axis=0, keepdims=True)
    e = jnp.exp(raw - m)
    sm = e / jnp.sum(e, axis=0, keepdims=True)
    cs = sm[0:1]
    for i in range(1, j + 1):
        cs = cs + sm[i:i + 1]
    return cs - sm[0:1]


def _hgrn_consts(c):
    t = np.arange(c)[:, None]
    u = np.arange(c)[None, :]
    mats = [u <= t, u > t]
    masks = [t == u]
    for lvl in range(1, int(np.log2(c)) + 1):
        m = 1 << lvl
        half = m >> 1
        mid = (t // m) * m + half
        upper = (t % m) >= half
        mats.append((upper & (u >= mid) & (u <= t)) | (~upper & (u > t) & (u <= mid - 1)))
        masks.append(((t // m) == (u // m)) & upper & ((u % m) < half))
    return (np.concatenate(mats, 0).astype(np.float32), np.stack(masks).astype(np.float32))


def _hgrn_prompt_body(j, za_ref, lbraw_ref, hn_ref, mst_ref, lm_ref, o_ref, sfin_ref, st_ref):
    c = HGRN_CHUNK
    ci = pl.program_id(1)

    @pl.when(ci == 0)
    def _():
        st_ref[...] = jnp.zeros(st_ref.shape, F32)

    lb = _lower_bound(lbraw_ref[...], j)
    fg = lb + (1.0 - lb) * _sigmoid(za_ref[:, 512:1024])
    lf = jnp.log(fg)
    q = _silu(za_ref[:, 0:512])
    k = 1.0 - fg
    v = za_ref[:, 1024:1536]
    gg = _silu(za_ref[:, 1536:2048])
    nlev = lm_ref.shape[0]
    x_all = jnp.exp(_dot_01_lhs(mst_ref[...], lf))
    for h in range(H_A):
        sl = slice(h * DK_A, (h + 1) * DK_A)
        qh, kh, vh = q[:, sl], k[:, sl], v[:, sl]
        x = x_all[:, sl]
        att = lm_ref[0] * _dot_nt(qh.astype(BF16), kh.astype(BF16))
        for lvl in range(1, nlev):
            d = x[(1 + lvl) * c:(2 + lvl) * c]
            att = att + lm_ref[lvl] * _dot_nt((qh * d).astype(BF16), (kh * d).astype(BF16))
        st = st_ref[h]
        o = _dot_nt((qh * x[0:c]).astype(BF16), st.astype(BF16)) + _dot(att.astype(BF16), vh.astype(BF16))
        kout = (kh * x[c:2 * c]).astype(BF16)
        st_new = st * x[c - 1:c] + _dot(vh.T.astype(BF16), kout)
        st_ref[h] = st_new
        on = _rms(o, hn_ref[...]) * gg[:, sl]
        o_ref[:, sl] = on.astype(o_ref.dtype)

        @pl.when(ci == pl.num_programs(1) - 1)
        def _():
            sfin_ref[h] = st_new.T


def hgrn_prompt(za, lb_raw, hnorm, j, batch):
    m = za.shape[0]
    t = m // batch
    c = HGRN_CHUNK
    nchunk = t // c
    mst, lm = _hgrn_consts(c)
    return pl.pallas_call(
        functools.partial(_hgrn_prompt_body, j),
        grid=(batch, nchunk),
        in_specs=[pl.BlockSpec((c, 2048), lambda b, i: (b * nchunk + i, 0)),
                  _const_spec(lb_raw.shape),
                  _const_spec((1, DK_A)),
                  _const_spec(mst.shape),
                  _const_spec(lm.shape)],
        out_specs=[pl.BlockSpec((c, D_A), lambda b, i: (b * nchunk + i, 0)),
                   pl.BlockSpec((None, H_A, DK_A, DK_A), lambda b, i: (b, 0, 0, 0))],
        out_shape=[jax.ShapeDtypeStruct((m, D_A), BF16),
                   jax.ShapeDtypeStruct((batch, H_A, DK_A, DK_A), F32)],
        scratch_shapes=[pltpu.VMEM((H_A, DK_A, DK_A), F32)],
        compiler_params=_cparams(("parallel", "arbitrary"), 48),
        name="hgrn_prompt",
    )(za, lb_raw, hnorm.reshape(1, DK_A), jnp.asarray(mst, BF16), jnp.asarray(lm, F32))


def _hgrn_step_body(j, bt, za_ref, s_ref, lbraw_ref, hn_ref, o_ref, snew_ref):
    lb = _lower_bound(lbraw_ref[...], j)
    fg = lb + (1.0 - lb) * _sigmoid(za_ref[:, 512:1024])
    q = _silu(za_ref[:, 0:512])
    k = 1.0 - fg
    v = za_ref[:, 1024:1536]
    gg = _silu(za_ref[:, 1536:2048])
    zpad = jnp.zeros((LANES - bt, DK_A), F32)

    def cols(a):
        return jnp.concatenate([a, zpad], axis=0).T

    for h in range(H_A):
        sl = slice(h * DK_A, (h + 1) * DK_A)
        qc, fc, kc = cols(q[:, sl]), cols(fg[:, sl]), cols(k[:, sl])
        for b in range(bt):
            s_new = fc[:, b:b + 1] * s_ref[b, h] + kc[:, b:b + 1] * v[b:b + 1, sl]
            snew_ref[b, h] = s_new
            o = jnp.sum(qc[:, b:b + 1] * s_new, axis=0, keepdims=True)
            o_ref[b:b + 1, sl] = (_rms(o, hn_ref[...]) * gg[b:b + 1, sl]).astype(o_ref.dtype)


def hgrn_step(za, state, lb_raw, hnorm, j):
    nb = za.shape[0]
    bt = 8
    return pl.pallas_call(
        functools.partial(_hgrn_step_body, j, bt),
        grid=(nb // bt,),
        in_specs=[pl.BlockSpec((bt, 2048), lambda i: (i, 0)),
                  pl.BlockSpec((None, bt, H_A, DK_A, DK_A), lambda i: (j, i, 0, 0, 0)),
                  _const_spec(lb_raw.shape),
                  _const_spec((1, DK_A))],
        out_specs=[pl.BlockSpec((bt, D_A), lambda i: (i, 0)),
                   pl.BlockSpec((bt, H_A, DK_A, DK_A), lambda i: (i, 0, 0, 0))],
        out_shape=[jax.ShapeDtypeStruct((nb, D_A), BF16),
                   jax.ShapeDtypeStruct(state.shape[1:], F32)],
        compiler_params=_cparams(("parallel",)),
        name="hgrn_step",
    )(za, state, lb_raw, hnorm.reshape(1, DK_A))


def _compress_weights(pe, w1, w2):
    eye = jnp.eye(N_KV_B, dtype=F32)
    w1r = w1.reshape(2, CMP_BLOCK, HD_B, CMP_HIDDEN)
    w1x = jnp.einsum("krdh,gG->rkgdGh", w1r, eye).reshape(CMP_BLOCK, 2, N_KV_B * HD_B, N_KV_B * CMP_HIDDEN)
    pex = jnp.broadcast_to(pe.transpose(1, 0, 2)[:, :, None, :], (CMP_BLOCK, 2, N_KV_B, HD_B))
    pex = pex.reshape(CMP_BLOCK, 2, 1, N_KV_B * HD_B)
    w2xt = jnp.einsum("khd,gG->kGdgh", w2, eye).reshape(2, N_KV_B * HD_B, N_KV_B * CMP_HIDDEN)
    return w1x.astype(BF16), pex.astype(F32), w2xt.astype(BF16)


def _compress_body(npp, nseg, *refs):
    pages = refs[1:1 + npp]
    w1_ref, pe_ref, w2_ref, o_ref, xk, xv, bbuf = refs[1 + npp:]
    s = pl.program_id(1)
    for i, pg in enumerate(pages):
        row0 = pl.multiple_of((s * npp + i) * PAGE_SIZE, PAGE_SIZE)
        xk[pl.ds(row0, PAGE_SIZE), :] = pg[0].T
        xv[pl.ds(row0, PAGE_SIZE), :] = pg[1].T

    @pl.when(s == pl.num_programs(1) - 1)
    def _():
        bbuf[pl.ds(nseg, 8), :] = jnp.zeros((8, N_KV_B * CMP_HIDDEN), F32)
        for kv, xbuf in enumerate((xk, xv)):
            acc_a = jnp.zeros((nseg, N_KV_B * CMP_HIDDEN), F32)
            acc_b = jnp.zeros((nseg, N_KV_B * CMP_HIDDEN), F32)
            for r in range(CMP_STRIDE):
                xr = xbuf[pl.ds(r, nseg, stride=CMP_STRIDE), :]
                acc_a = acc_a + _dot((xr + pe_ref[r, kv]).astype(BF16), w1_ref[r, kv])
                acc_b = acc_b + _dot((xr + pe_ref[CMP_STRIDE + r, kv]).astype(BF16), w1_ref[CMP_STRIDE + r, kv])
            bbuf[pl.ds(0, nseg), :] = acc_b
            hid = _gelu(acc_a + bbuf[pl.ds(1, nseg), :])
            o_ref[kv] = _dot_nt(w2_ref[kv], hid.astype(BF16))


def compress_pages(table, pages_t, j, lane_paged, w1x, pex, w2xt, npp):
    nb, npg = table.shape
    nseg = npg * PAGE_SIZE // CMP_STRIDE
    half = KV_COLS // 2
    if lane_paged:
        imap = lambda b, s, t, i: (j, b, 0, 0, t[jnp.minimum(b, nb - 1), s * npp + i])
    else:
        imap = lambda b, s, t, i: (j, t[jnp.minimum(b, nb - 1), s * npp + i], 0, 0, 0)
    page_specs = [pl.BlockSpec((None, None, 2, half, PAGE_SIZE), functools.partial(imap, i=i)) for i in range(npp)]
    grid_spec = pltpu.PrefetchScalarGridSpec(
        num_scalar_prefetch=1,
        grid=(nb, npg // npp),
        in_specs=page_specs + [_const_spec(w1x.shape), _const_spec(pex.shape), _const_spec(w2xt.shape)],
        out_specs=pl.BlockSpec((None, 2, half, nseg), lambda b, s, t: (b, 0, 0, 0)),
        scratch_shapes=[pltpu.VMEM((npg * PAGE_SIZE, half), F32),
                        pltpu.VMEM((npg * PAGE_SIZE, half), F32),
                        pltpu.VMEM((nseg + 8, N_KV_B * CMP_HIDDEN), F32)],
    )
    return pl.pallas_call(
        functools.partial(_compress_body, npp, nseg),
        grid_spec=grid_spec,
        out_shape=jax.ShapeDtypeStruct((nb, 2, half, nseg), F32),
        compiler_params=_cparams(("parallel", "arbitrary"), 56),
        name="compress_pages",
    )(table, *([pages_t] * npp), w1x, pex, w2xt)


def _overlap_matrix(n_cmp_pad, n_sel_pad, n_cmp, n_sel):
    cs = np.arange(n_cmp_pad)[:, None] * CMP_STRIDE
    ss = np.arange(n_sel_pad)[None, :] * SEL_BLOCK
    ov = (cs < ss + SEL_BLOCK) & (cs + CMP_BLOCK > ss)
    ov &= (np.arange(n_cmp_pad)[:, None] < n_cmp) & (np.arange(n_sel_pad)[None, :] < n_sel)
    return ov.astype(np.float32)


def _rank(imp, jidx, cur):
    valid = jidx <= cur
    dist = cur - jidx
    forced = (jidx == 0) | ((dist >= 0) & (dist < N_LOCAL))
    return jnp.where(valid, imp + BIG * forced.astype(F32), -BIG)


NSA_TQ = 256
NSA_TK = 512
NSA_TW = 256
NSA_RB = 64


def _nsa_prompt_body(n_cmp, n_sel, q_ref, kct_ref, vct_ref, kst_ref, vst_ref, kwt_ref, vwt_ref, gt_ref, ov_ref,
                     eb_ref, o_ref, kc2, vc2e, vc2o, k2e, k2o, v2e, v2o, kw2, vw2e, vw2o, s_sc, e_sc, m_sc, acc_sc):
    tq, tk, tw, rb_ = NSA_TQ, NSA_TK, NSA_TW, NSA_RB
    hd = HD_B
    g = pl.program_id(1)
    qi = pl.program_id(2)
    t0 = qi * tq
    t_len = kst_ref.shape[1]
    ncp = kct_ref.shape[1]

    @pl.when(qi == 0)
    def _():
        zc = jnp.zeros((hd, ncp), BF16)
        kct = kct_ref[...].astype(BF16)
        vct = vct_ref[...].astype(BF16)
        kc2[0:hd] = kct
        kc2[hd:2 * hd] = kct
        vc2e[0:hd] = vct
        vc2e[hd:2 * hd] = zc
        vc2o[0:hd] = zc
        vc2o[hd:2 * hd] = vct
        ones_row = (lax.broadcasted_iota(I32, (hd, tk), 0) == 0).astype(BF16)
        for kt in range(t_len // tk):
            cols = slice(kt * tk, (kt + 1) * tk)
            kt_ = kst_ref[:, cols].astype(BF16)
            vt_ = vst_ref[:, cols].astype(BF16)
            k2e[kt, 0:hd] = kt_
            k2e[kt, hd:2 * hd] = eb_ref[kt]
            k2o[kt, 0:hd] = eb_ref[kt]
            k2o[kt, hd:2 * hd] = kt_
            v2e[kt, 0:hd] = vt_
            v2e[kt, hd:2 * hd] = ones_row
            v2o[kt, 0:hd] = ones_row
            v2o[kt, hd:2 * hd] = vt_
        zw = jnp.zeros((hd, tw), BF16)
        for wt in range(t_len // tw):
            cols = slice(wt * tw, (wt + 1) * tw)
            kt_ = kwt_ref[:, cols].astype(BF16)
            vt_ = vwt_ref[:, cols].astype(BF16)
            kw2[wt, 0:hd] = kt_
            kw2[wt, hd:2 * hd] = kt_
            vw2e[wt, 0:hd] = vt_
            vw2e[wt, hd:2 * hd] = zw
            vw2o[wt, 0:hd] = zw
            vw2o[wt, hd:2 * hd] = vt_

    lane = lax.broadcasted_iota(I32, (1, LANES), 1)
    even = lane < hd
    tpos = t0 + lax.broadcasted_iota(I32, (tq, 1), 0)
    gts = _sigmoid(gt_ref[...])
    gsel = jnp.where(g == 0, gts[:, 0:REP_B * 3], gts[:, REP_B * 3:2 * REP_B * 3])
    gate = lambda r, c: gsel[:, 3 * r + c:3 * r + c + 1]
    zq = jnp.zeros((tq, LANES), q_ref.dtype)
    q128 = [q_ref[:, p * LANES:(p + 1) * LANES] * ATT_SCALE for p in range(REP_B // 2)]
    qh = [jnp.where(even if r % 2 == 0 else ~even, q128[r // 2], zq) for r in range(REP_B)]

    n_idx = lax.broadcasted_iota(I32, (1, ncp), 1)
    cmask = ((n_idx * CMP_STRIDE + (CMP_BLOCK - 1)) <= tpos) & (n_idx < n_cmp)
    o_cmp = []
    psum = jnp.zeros((tq, ncp), F32)
    for r in range(REP_B):
        p = _softmax_masked(_dot(qh[r], kc2[...]), cmask)
        psum = psum + p
        o_cmp.append(_dot_nt(p.astype(BF16), (vc2e if r % 2 == 0 else vc2o)[...]))
    imp = _dot_01_rhs(psum, ov_ref[...])

    imp_t = imp.T[0:n_sel]
    jidx = lax.broadcasted_iota(I32, (n_sel, tq), 0)
    cur = jnp.right_shift(t0 + lax.broadcasted_iota(I32, (n_sel, tq), 1), 6)
    rank = _rank(imp_t, jidx, cur)
    cnt = jnp.zeros((n_sel, tq), I32)
    for i in range(n_sel):
        ri = rank[i:i + 1]
        beats = (ri > rank) | ((ri == rank) & (jidx > i))
        cnt = cnt + beats.astype(I32)
    drop_t = (cnt >= min(N_SEL, n_sel)).astype(F32)
    if n_sel < hd:
        drop_t = jnp.concatenate([drop_t, jnp.zeros((hd - n_sel, tq), F32)], axis=0)
    drop = jnp.concatenate([drop_t, drop_t], axis=0).T.astype(q_ref.dtype)
    q2 = [jnp.where(even if r % 2 == 0 else ~even, q128[r // 2], drop) for r in range(REP_B)]

    m_sc[...] = jnp.full(m_sc.shape, NEG_MASK, F32)
    acc_sc[...] = jnp.zeros(acc_sc.shape, F32)

    def kv_tile(kt, causal):
        k0 = kt * tk
        kpos = k0 + lax.broadcasted_iota(I32, (1, tk), 1)
        for r in range(REP_B):
            s_sc[r] = _dot(q2[r], (k2e if r % 2 == 0 else k2o)[kt])
        for r in range(REP_B):
            for rb in range(tq // rb_):
                rows = slice(rb * rb_, (rb + 1) * rb_)
                sb = s_sc[r, rows]
                if causal:
                    sb = jnp.where(kpos <= tpos[rows], sb, NEG_MASK)
                m_old = m_sc[r, rows]
                m_new = jnp.maximum(m_old, jnp.max(sb, axis=-1, keepdims=True))
                m_sc[r, rows] = m_new
                e_sc[r, rows] = jnp.exp(sb - m_new).astype(e_sc.dtype)
                acc_sc[r, rows] = jnp.exp(m_old - m_new) * acc_sc[r, rows]
        for r in range(REP_B):
            acc_sc[r] += _dot_nt(e_sc[r], (v2e if r % 2 == 0 else v2o)[kt])

    n_full = t0 // tk

    def full_tile(kt, carry):
        kv_tile(kt, False)
        return carry

    lax.fori_loop(0, n_full, full_tile, 0)
    kv_tile(n_full, True)
    o_sel = []
    for r in range(REP_B):
        a = acc_sc[r]
        own = even if r % 2 == 0 else ~even
        l = a[:, hd:hd + 1] if r % 2 == 0 else a[:, 0:1]
        o_sel.append(jnp.where(own, a / jnp.maximum(l, 1e-30), 0.0))

    nwt = (WINDOW + tq) // tw
    w0 = jnp.maximum(t0 - WINDOW, 0)
    wt0 = w0 // tw
    dist = tpos - (w0 + lax.broadcasted_iota(I32, (1, nwt * tw), 1))
    wmask = (dist >= 0) & (dist < WINDOW)
    o_win = []
    for r in range(REP_B):
        sw = jnp.concatenate([_dot(qh[r], kw2[wt0 + c]) for c in range(nwt)], axis=1)
        pw = _softmax_masked(sw, wmask).astype(BF16)
        vw = vw2e if r % 2 == 0 else vw2o
        ow = _dot_nt(pw[:, 0:tw], vw[wt0])
        for c in range(1, nwt):
            ow = ow + _dot_nt(pw[:, c * tw:(c + 1) * tw], vw[wt0 + c])
        o_win.append(ow)

    for p in range(REP_B // 2):
        acc = jnp.zeros((tq, LANES), F32)
        for r in (2 * p, 2 * p + 1):
            acc = acc + gate(r, 0) * o_cmp[r] + gate(r, 1) * o_sel[r] + gate(r, 2) * o_win[r]
        o_ref[:, p * LANES:(p + 1) * LANES] = acc.astype(o_ref.dtype)


def nsa_prompt(q, kct, kvt_sel, kvt_win, gates, j, batch, n_cmp):
    m = q.shape[0]
    t = m // batch
    tq, tk, tw = NSA_TQ, NSA_TK, NSA_TW
    hd = HD_B
    n_sel = t // SEL_BLOCK
    ncp = kct.shape[3]
    nqt = t // tq
    ov = jnp.asarray(_overlap_matrix(ncp, LANES, n_cmp, n_sel), BF16)
    kb = np.arange(t) // SEL_BLOCK
    eb = np.where(np.arange(hd)[:, None] == kb[None, :], NEG_MASK, 0.0).astype(np.float32)
    eb = jnp.asarray(eb.reshape(hd, t // tk, tk).transpose(1, 0, 2), BF16)
    n_ab = kvt_sel.shape[0]
    kct4 = kct.reshape(batch, 2 * N_KV_B, hd, ncp)
    sel5 = kvt_sel.reshape(n_ab, batch, 2 * N_KV_B, hd, t)
    win5 = kvt_win.reshape(n_ab, batch, 2 * N_KV_B, hd, t)
    cspec = lambda kv: pl.BlockSpec((None, None, hd, ncp), lambda b, g, i: (b, kv * N_KV_B + g, 0, 0))
    rspec = lambda kv: pl.BlockSpec((None, None, None, hd, t), lambda b, g, i: (j, b, kv * N_KV_B + g, 0, 0))
    slab = lambda n, w: pltpu.VMEM((n, 2 * hd, w), BF16)
    return pl.pallas_call(
        functools.partial(_nsa_prompt_body, n_cmp, n_sel),
        grid=(batch, N_KV_B, nqt),
        in_specs=[pl.BlockSpec((tq, REP_B * hd), lambda b, g, i: (b * nqt + i, g)),
                  cspec(0), cspec(1), rspec(0), rspec(1), rspec(0), rspec(1),
                  pl.BlockSpec((tq, LANES), lambda b, g, i: (b * nqt + i, 0)),
                  _const_spec(ov.shape), _const_spec(eb.shape)],
        out_specs=pl.BlockSpec((tq, REP_B * hd), lambda b, g, i: (b * nqt + i, g)),
        out_shape=jax.ShapeDtypeStruct((m, N_KV_B * REP_B * hd), BF16),
        scratch_shapes=[pltpu.VMEM((2 * hd, ncp), BF16), pltpu.VMEM((2 * hd, ncp), BF16), pltpu.VMEM((2 * hd, ncp), BF16),
                        slab(t // tk, tk), slab(t // tk, tk), slab(t // tk, tk), slab(t // tk, tk),
                        slab(t // tw, tw), slab(t // tw, tw), slab(t // tw, tw),
                        pltpu.VMEM((REP_B, tq, tk), F32), pltpu.VMEM((REP_B, tq, tk), BF16),
                        pltpu.VMEM((REP_B, tq, 1), F32), pltpu.VMEM((REP_B, tq, LANES), F32)],
        compiler_params=_cparams(("parallel", "parallel", "arbitrary"), 56),
        name="nsa_prompt",
    )(q, kct4, kct4, sel5, sel5, win5, win5, gates, ov, eb)


def _group_queries(q8, g):
    z = jnp.zeros_like(q8)
    return jnp.concatenate([q8, z] if g == 0 else [z, q8], axis=1)


def _nsa_step_select_body(q_pos, n_cmp, n_sel, q_ref, kct_ref, ov_ref, ocmp_ref, idx_ref):
    q8 = q_ref[...] * ATT_SCALE
    ncp = kct_ref.shape[2]
    nsp = ov_ref.shape[1]
    n_idx = lax.broadcasted_iota(I32, (1, ncp), 1)
    cmask = ((n_idx * CMP_STRIDE + (CMP_BLOCK - 1)) <= q_pos) & (n_idx < n_cmp)
    head = lax.broadcasted_iota(I32, (N_KV_B * REP_B, 1), 0)
    kct = kct_ref[0].astype(BF16)
    vct = kct_ref[1].astype(BF16)
    o_all = jnp.zeros((N_KV_B * REP_B, HD_B), F32)
    for g in range(N_KV_B):
        ingrp = (head // REP_B) == g
        p = _softmax_masked(_dot(_group_queries(q8, g), kct), cmask)
        o_g = _dot_nt(p.astype(BF16), vct)
        o_all = jnp.where(ingrp, o_g[:, g * HD_B:(g + 1) * HD_B], o_all)
        imp8 = _dot_01_rhs(jnp.where(ingrp, p, 0.0), ov_ref[...])
        imp = jnp.sum(imp8, axis=0, keepdims=True)
        lane_j = lax.broadcasted_iota(I32, (nsp, nsp), 1)
        sub_i = lax.broadcasted_iota(I32, (nsp, nsp), 0)
        cur = q_pos // SEL_BLOCK
        rank_row = jnp.broadcast_to(_rank(imp, lane_j[0:1], cur), (nsp, nsp))
        rank_col = rank_row.T
        before = (rank_row > rank_col) | ((rank_row == rank_col) & (lane_j < sub_i))
        order = jnp.sum(before.astype(F32), axis=1, keepdims=True)
        slot = lax.broadcasted_iota(I32, (nsp, LANES), 1).astype(F32)
        hit = (order == slot) & (sub_i[:, 0:1] < n_sel)
        blk = lax.broadcasted_iota(I32, (nsp, LANES), 0).astype(F32)
        idx = jnp.sum(jnp.where(hit, blk, 0.0), axis=0, keepdims=True)
        idx_ref[g:g + 1, :] = idx.astype(I32)
    ocmp_ref[...] = o_all


def nsa_step_select(q, kct, n_cmp, n_sel, q_pos):
    nb = q.shape[0]
    ncp = kct.shape[3]
    nsp = 256
    nh = N_KV_B * REP_B
    ov = jnp.asarray(_overlap_matrix(ncp, nsp, n_cmp, n_sel), BF16)
    return pl.pallas_call(
        functools.partial(_nsa_step_select_body, q_pos, n_cmp, n_sel),
        grid=(nb,),
        in_specs=[pl.BlockSpec((None, nh, HD_B), lambda b: (b, 0, 0)),
                  pl.BlockSpec((None, 2, KV_COLS // 2, ncp), lambda b: (b, 0, 0, 0)),
                  _const_spec(ov.shape)],
        out_specs=[pl.BlockSpec((None, nh, HD_B), lambda b: (b, 0, 0)),
                   pl.BlockSpec((None, N_KV_B, LANES), lambda b: (b, 0, 0))],
        out_shape=[jax.ShapeDtypeStruct((nb, nh, HD_B), F32),
                   jax.ShapeDtypeStruct((nb, N_KV_B, LANES), I32)],
        compiler_params=_cparams(("parallel",)),
        name="nsa_step_select",
    )(q, kct, ov)


def _attend_with_new(qp, kt, vt, kmask, k_new, v_new, new_ok):
    s = _dot(qp, kt)
    s_new = jnp.sum(qp.astype(F32) * k_new, axis=1, keepdims=True)
    sm = jnp.where(kmask, s, NEG)
    sn = jnp.where(new_ok, s_new, NEG)
    m = jnp.maximum(jnp.max(sm, axis=-1, keepdims=True), sn)
    e = jnp.where(kmask, jnp.exp(sm - m), 0.0)
    e_new = jnp.where(new_ok, jnp.exp(sn - m), 0.0)
    l = jnp.sum(e, axis=-1, keepdims=True) + e_new
    return (_dot_nt(e.astype(BF16), vt) + e_new * v_new) / jnp.maximum(l, 1e-30)


def _nsa_step_attend_body(q_pos, past_len, k_top, *refs):
    tbl_ref, idx_ref = refs[0], refs[1]
    pages = refs[2:2 + N_KV_B * k_top]
    (q_ref, ksn_ref, win_ref, kwn_ref, ocmp_ref, gt_ref, o_ref, kt_sc, vt_sc) = refs[2 + N_KV_B * k_top:]
    b = pl.program_id(0)
    nh = N_KV_B * REP_B
    half = KV_COLS // 2
    q8 = q_ref[...] * ATT_SCALE
    head = lax.broadcasted_iota(I32, (nh, 1), 0)
    nkeys = k_top * PAGE_SIZE
    wlen = win_ref.shape[2]
    wrow = lax.broadcasted_iota(I32, (1, wlen), 1)
    wdist = q_pos - (past_len - wlen + wrow)
    wmask = (wdist >= 0) & (wdist < WINDOW)
    kwt = win_ref[0].astype(BF16)
    vwt = win_ref[1].astype(BF16)
    ks_new, vs_new = ksn_ref[:, 0:half], ksn_ref[:, half:KV_COLS]
    kw_new, vw_new = kwn_ref[:, 0:half], kwn_ref[:, half:KV_COLS]
    o_sel = jnp.zeros((nh, HD_B), F32)
    o_win = jnp.zeros((nh, HD_B), F32)
    for g in range(N_KV_B):
        ingrp = (head // REP_B) == g
        qp = _group_queries(q8, g)
        has_new = jnp.zeros((), I32)
        krow = lax.broadcasted_iota(I32, (1, nkeys), 1)
        kmask = jnp.zeros((1, nkeys), jnp.bool_)
        for i in range(k_top):
            pg = pages[g * k_top + i]
            kt_sc[:, i * PAGE_SIZE:(i + 1) * PAGE_SIZE] = pg[0].astype(BF16)
            vt_sc[:, i * PAGE_SIZE:(i + 1) * PAGE_SIZE] = pg[1].astype(BF16)
            jb = idx_ref[b, g, i]
            kpos = (jb * SEL_BLOCK // PAGE_SIZE) * PAGE_SIZE + (krow - i * PAGE_SIZE)
            inblk = (krow >= i * PAGE_SIZE) & (krow < (i + 1) * PAGE_SIZE) & (kpos // SEL_BLOCK == jb)
            kmask = kmask | (inblk & (kpos <= q_pos) & (kpos < past_len))
            has_new = has_new | (jb * SEL_BLOCK + SEL_BLOCK > past_len).astype(I32)
        os_ = _attend_with_new(qp, kt_sc[...], vt_sc[...], kmask, ks_new, vs_new, has_new > 0)
        o_sel = jnp.where(ingrp, os_[:, g * HD_B:(g + 1) * HD_B], o_sel)
        ow_ = _attend_with_new(qp, kwt, vwt, wmask, kw_new, vw_new, True)
        o_win = jnp.where(ingrp, ow_[:, g * HD_B:(g + 1) * HD_B], o_win)
    gt = _sigmoid(gt_ref[...])
    o_ref[...] = (gt[:, 0:1] * ocmp_ref[...] + gt[:, 1:2] * o_sel + gt[:, 2:3] * o_win).astype(o_ref.dtype)


def nsa_step_attend(table, idx, cache_sel_t, win_t, j, q, kvs_new, kvw_new, o_cmp, gates, q_pos, past_len):
    nb = q.shape[0]
    k_top = N_SEL
    wlen = win_t.shape[4]
    half = KV_COLS // 2
    per_page = PAGE_SIZE // SEL_BLOCK
    last_cached = past_len // SEL_BLOCK - 1

    def page_map(b, t, ix, g, i):
        bb = jnp.minimum(b, nb - 1)
        jb = jnp.minimum(ix[bb, g, i], last_cached)
        return (j, t[bb, jb // per_page], 0, 0, 0)

    page_specs = [pl.BlockSpec((None, None, 2, half, PAGE_SIZE), functools.partial(page_map, g=g, i=i))
                  for g in range(N_KV_B) for i in range(k_top)]
    nh = N_KV_B * REP_B
    row_spec = lambda w: pl.BlockSpec((None, nh, w), lambda b, t, ix: (b, 0, 0))
    new_spec = pl.BlockSpec((None, 1, KV_COLS), lambda b, t, ix: (b, 0, 0))
    grid_spec = pltpu.PrefetchScalarGridSpec(
        num_scalar_prefetch=2,
        grid=(nb,),
        in_specs=page_specs + [row_spec(HD_B), new_spec,
                               pl.BlockSpec((None, None, 2, half, wlen), lambda b, t, ix: (j, b, 0, 0, 0)),
                               new_spec, row_spec(HD_B), row_spec(3)],
        out_specs=row_spec(HD_B),
        scratch_shapes=[pltpu.VMEM((half, k_top * PAGE_SIZE), BF16),
                        pltpu.VMEM((half, k_top * PAGE_SIZE), BF16)],
    )
    return pl.pallas_call(
        functools.partial(_nsa_step_attend_body, q_pos, past_len, k_top),
        grid_spec=grid_spec,
        out_shape=jax.ShapeDtypeStruct((nb, nh, HD_B), BF16),
        compiler_params=_cparams(("arbitrary",)),
        name="nsa_step_attend",
    )(table, idx, *([cache_sel_t] * (N_KV_B * k_top)), q, kvs_new, win_t, kvw_new, o_cmp, gates)


def _out_proj_body(oa_ref, ob_ref, w_ref, r_ref, o_ref):
    o_ref[...] = r_ref[...] + _dot(oa_ref[...], w_ref[0:D_A, :]) + _dot(ob_ref[...], w_ref[D_A:D_A + D_B, :])


def out_proj(o_a, o_b, w, resid):
    m = resid.shape[0]
    tm = min(m, 512)
    return pl.pallas_call(
        _out_proj_body,
        grid=(m // tm,),
        in_specs=[pl.BlockSpec((tm, D_A), lambda i: (i, 0)), pl.BlockSpec((tm, D_B), lambda i: (i, 0)),
                  _const_spec(w.shape), pl.BlockSpec((tm, D_MODEL), lambda i: (i, 0))],
        out_specs=pl.BlockSpec((tm, D_MODEL), lambda i: (i, 0)),
        out_shape=jax.ShapeDtypeStruct((m, D_MODEL), F32),
        compiler_params=_cparams(("parallel",)),
        name="out_proj",
    )(o_a, o_b, w, resid)


def _ffn_body(final, x_ref, g_ref, w1_ref, w2_ref, gf_ref, o_ref, xn_sc, acc_sc):
    jf = pl.program_id(1)

    @pl.when(jf == 0)
    def _():
        xn_sc[...] = _rms(x_ref[...], g_ref[...]).astype(BF16)
        acc_sc[...] = jnp.zeros(acc_sc.shape, F32)

    h = jnp.maximum(_dot(xn_sc[...], w1_ref[...]), 0.0)
    acc_sc[...] += _dot((h * h).astype(BF16), w2_ref[...])

    @pl.when(jf == pl.num_programs(1) - 1)
    def _():
        y = x_ref[...] + acc_sc[...]
        o_ref[...] = _rms(y, gf_ref[...]) if final else y


def ffn(x, g, w1, w2, g_final, final):
    m = x.shape[0]
    tm = min(m, 1024)
    tf = 512
    return pl.pallas_call(
        functools.partial(_ffn_body, final),
        grid=(m // tm, D_FF // tf),
        in_specs=[pl.BlockSpec((tm, D_MODEL), lambda i, j: (i, 0)),
                  _const_spec((1, D_MODEL)),
                  pl.BlockSpec((D_MODEL, tf), lambda i, j: (0, j)),
                  pl.BlockSpec((tf, D_MODEL), lambda i, j: (j, 0)),
                  _const_spec((1, D_MODEL))],
        out_specs=pl.BlockSpec((tm, D_MODEL), lambda i, j: (i, 0)),
        out_shape=jax.ShapeDtypeStruct((m, D_MODEL), F32),
        scratch_shapes=[pltpu.VMEM((tm, D_MODEL), BF16), pltpu.VMEM((tm, D_MODEL), F32)],
        compiler_params=_cparams(("parallel", "arbitrary"), 48),
        name="ffn",
    )(x, g.reshape(1, D_MODEL), w1, w2, g_final.reshape(1, D_MODEL))


def _layernorm(v, g, b):
    mu = jnp.mean(v, axis=-1, keepdims=True)
    var = jnp.mean(jnp.square(v - mu), axis=-1, keepdims=True)
    return (v - mu) * lax.rsqrt(var + EPS) * g + b


def _gmlp_prompt_body(tm, x_ref, g_ref, win_ref, lng_ref, lnb_ref, ws_ref, bs_ref, wout_ref, o_ref, mix_sc):
    x = x_ref[...]
    z = _gelu(_dot(_rms(x, g_ref[...]).astype(BF16), win_ref[...]))
    u = z[:, 0:D_MODEL]
    v = _layernorm(z[:, D_MODEL:2 * D_MODEL], lng_ref[...], lnb_ref[...]).astype(BF16)
    row = lax.broadcasted_iota(I32, (CHUNK_C, CHUNK_C), 0)
    col = lax.broadcasted_iota(I32, (CHUNK_C, CHUNK_C), 1)
    gcd = D_MODEL // G_C
    for gi in range(G_C):
        ws = jnp.where(row >= col, ws_ref[gi], 0.0).astype(BF16)
        for ch in range(tm // CHUNK_C):
            rows = slice(ch * CHUNK_C, (ch + 1) * CHUNK_C)
            cols = slice(gi * gcd, (gi + 1) * gcd)
            mix_sc[rows, cols] = _dot(ws, v[rows, cols]) + bs_ref[gi]
    o_ref[...] = x + _dot((u * mix_sc[...]).astype(BF16), wout_ref[...])


def gmlp_prompt(x, g, w_in, ln_g, ln_b, w_s, b_s, w_out):
    m = x.shape[0]
    tm = 256
    return pl.pallas_call(
        functools.partial(_gmlp_prompt_body, tm),
        grid=(m // tm,),
        in_specs=[pl.BlockSpec((tm, D_MODEL), lambda i: (i, 0)),
                  _const_spec((1, D_MODEL)), _const_spec(w_in.shape),
                  _const_spec((1, D_MODEL)), _const_spec((1, D_MODEL)),
                  _const_spec(w_s.shape), _const_spec((G_C, CHUNK_C, 1)), _const_spec(w_out.shape)],
        out_specs=pl.BlockSpec((tm, D_MODEL), lambda i: (i, 0)),
        out_shape=jax.ShapeDtypeStruct((m, D_MODEL), F32),
        scratch_shapes=[pltpu.VMEM((tm, D_MODEL), F32)],
        compiler_params=_cparams(("parallel",), 48),
        name="gmlp_prompt",
    )(x, g.reshape(1, D_MODEL), w_in, ln_g.reshape(1, D_MODEL), ln_b.reshape(1, D_MODEL),
      w_s, b_s.reshape(G_C, CHUNK_C, 1), w_out)


def _gmlp_step_body(x_ref, g_ref, win_ref, lng_ref, lnb_ref, sc_ref, bi_ref, wout_ref, o_ref, v_ref):
    x = x_ref[...]
    z = _gelu(_dot(_rms(x, g_ref[...]).astype(BF16), win_ref[...]))
    v = _layernorm(z[:, D_MODEL:2 * D_MODEL], lng_ref[...], lnb_ref[...])
    v_ref[...] = v
    mix = v * sc_ref[...] + bi_ref[...]
    o_ref[...] = x + _dot((z[:, 0:D_MODEL] * mix).astype(BF16), wout_ref[...])


def gmlp_step(x, g, w_in, ln_g, ln_b, w_s, b_s, w_out):
    m = x.shape[0]
    gcd = D_MODEL // G_C
    scale = jnp.repeat(w_s[:, 0, 0], gcd).reshape(1, D_MODEL)
    bias = jnp.repeat(b_s[:, 0], gcd).reshape(1, D_MODEL)
    row = _const_spec((1, D_MODEL))
    full = _const_spec((m, D_MODEL))
    return pl.pallas_call(
        _gmlp_step_body,
        grid=(1,),
        in_specs=[full, row, _const_spec(w_in.shape), row, row, row, row, _const_spec(w_out.shape)],
        out_specs=[full, full],
        out_shape=[jax.ShapeDtypeStruct((m, D_MODEL), F32), jax.ShapeDtypeStruct((m, D_MODEL), F32)],
        compiler_params=_cparams(("arbitrary",), 48),
        name="gmlp_step",
    )(x, g.reshape(1, D_MODEL), w_in, ln_g.reshape(1, D_MODEL), ln_b.reshape(1, D_MODEL), scale, bias, w_out)


def _rows_last(a):
    nd = a.ndim
    perm = tuple(range(nd - 4)) + (nd - 3, nd - 2, nd - 1, nd - 4)
    a = a.transpose(perm)
    return a.reshape(a.shape[:nd - 3] + (N_KV_B * HD_B, a.shape[-1]))


def kernel(x_prompt, x_sample, cache_cmp_kv, cache_sel_kv, state_win_kv, state_hgrn, page_table, norm_mix, norm_ffn, norm_final, w_in_ab, w_out_ab, hgrn_lower_bounds, hgrn_norm, cmp_pe, cmp_w1, cmp_w2, w_in_c, ln_c_g, ln_c_b, w_s, b_s, w_out_c, w_ffn1, w_ffn2):
    bp, seq, d = x_prompt.shape
    bs = x_sample.shape[0]
    depth = norm_mix.shape[0]
    n_ab = w_in_ab.shape[0]
    npg = page_table.shape[1]
    past_len = npg * PAGE_SIZE
    hp = x_prompt.reshape(bp * seq, d)
    hs = x_sample.reshape(bs, d)

    prompt_pages = jnp.broadcast_to(jnp.arange(seq // PAGE_SIZE, dtype=I32), (bp, seq // PAGE_SIZE))
    n_cmp_p = seq // CMP_STRIDE - 1
    n_cmp_s = (past_len + 1) // CMP_STRIDE - 1
    n_sel_s = -(-(past_len + 1) // SEL_BLOCK)
    cache_cmp_t = _rows_last(cache_cmp_kv)
    cache_sel_t = _rows_last(cache_sel_kv)
    win_t = _rows_last(state_win_kv)
    kv_ofs = AB_SEGS[2][0]

    kvt_p = None
    cmp_s, sel_s, win_s, hg_p, hg_s, cv_s = [], [], [], [], [], []
    for layer in range(depth):
        j = layer // 2
        if layer % 2 == 0:
            w_full = w_in_ab[j]
            w_in = jnp.pad(w_full, ((0, 0), (0, AB_COLS_PAD - AB_COLS))).astype(BF16)
            w_zqg = jnp.concatenate([w_in[:, :kv_ofs], w_in[:, kv_ofs + 3 * KV_COLS:]], axis=1)
            w_kv_t = w_full[:, kv_ofs:kv_ofs + 3 * KV_COLS].T.astype(BF16)
            w_out = w_out_ab[j].astype(BF16)
            w1x, pex, w2xt = _compress_weights(cmp_pe[j], cmp_w1[j], cmp_w2[j])

            za, q, gt, kc_t, ks_t, kw_t = in_proj_ab_t(hp, norm_mix[layer], w_zqg, w_kv_t, j, n_ab, bp, kvt_p)
            kvt_p = (kc_t, ks_t, kw_t)
            o_a, s_fin = hgrn_prompt(za, hgrn_lower_bounds, hgrn_norm[j], j, bp)
            kct = compress_pages(prompt_pages, kc_t.reshape(n_ab, bp, 2, KV_COLS // 2, seq), j, True,
                                 w1x, pex, w2xt, npp=min(16, seq // PAGE_SIZE))
            o_b = nsa_prompt(q, kct, ks_t, kw_t, gt, j, bp, n_cmp_p)
            hp = out_proj(o_a, o_b, w_out, hp)
            hg_p.append(s_fin)

            za, q, kvc, kvs, kvw, gt = in_proj_ab(hs, norm_mix[layer], w_in)
            o_a, s_new = hgrn_step(za, state_hgrn, hgrn_lower_bounds, hgrn_norm[j], j)
            kct = compress_pages(page_table, cache_cmp_t, j, False, w1x, pex, w2xt, npp=min(16, npg))
            q8 = q.reshape(bs, N_KV_B * REP_B, HD_B)
            o_cmp, idx = nsa_step_select(q8, kct, n_cmp_s, n_sel_s, past_len)
            o_b = nsa_step_attend(page_table, idx, cache_sel_t, win_t, j, q8, kvs.reshape(bs, 1, KV_COLS),
                                  kvw.reshape(bs, 1, KV_COLS), o_cmp,
                                  gt[:, :3 * N_KV_B * REP_B].reshape(bs, N_KV_B * REP_B, 3), past_len, past_len)
            hs = out_proj(o_a, o_b.reshape(bs, D_B), w_out, hs)
            kv6s = lambda a: a.reshape(bs, 1, 2, N_KV_B, HD_B)
            cmp_s.append(kv6s(kvc)); sel_s.append(kv6s(kvs)); hg_s.append(s_new)
            win_s.append(jnp.concatenate([state_win_kv[j], kv6s(kvw)], axis=1)[:, 1:])
        else:
            w_in = w_in_c[j].astype(BF16)
            w_out = w_out_c[j].astype(BF16)
            hp = gmlp_prompt(hp, norm_mix[layer], w_in, ln_c_g[j], ln_c_b[j], w_s[j], b_s[j], w_out)
            hs, v_new = gmlp_step(hs, norm_mix[layer], w_in, ln_c_g[j], ln_c_b[j], w_s[j], b_s[j], w_out)
            cv_s.append(v_new.reshape(bs, 1, d))
        w1 = w_ffn1[layer].astype(BF16)
        w2 = w_ffn2[layer].astype(BF16)
        final = layer == depth - 1
        hp = ffn(hp, norm_ffn[layer], w1, w2, norm_final, final)
        hs = ffn(hs, norm_ffn[layer], w1, w2, norm_final, final)

    rows_first = lambda a: a.reshape(n_ab, bp, 2, N_KV_B, HD_B, seq).transpose(0, 1, 5, 2, 3, 4)
    cmp_p, sel_p, win_p = (rows_first(a) for a in kvt_p)
    return (hp.reshape(bp, seq, d), hs.reshape(bs, 1, d), cmp_p, jnp.stack(cmp_s),
            sel_p, jnp.stack(sel_s), win_p[:, :, seq - min(WINDOW, seq):], jnp.stack(win_s),
            jnp.stack(hg_p), jnp.stack(hg_s), jnp.stack(cv_s))
```

```python
import functools

import numpy as np
import jax
import jax.numpy as jnp
from jax import lax
from jax.experimental import pallas as pl
from jax.experimental.pallas import tpu as pltpu

F32 = jnp.float32
BF16 = jnp.bfloat16
I32 = jnp.int32

D_MODEL = 1024
H_A = 4
DK_A = 128
D_A = 512
D_B = 512
HD_B = 64
N_KV_B = 2
REP_B = 4
CMP_BLOCK = 32
CMP_STRIDE = 16
CMP_HIDDEN = 128
SEL_BLOCK = 64
N_SEL = 16
N_LOCAL = 2
WINDOW = 512
PAGE_SIZE = 128
CHUNK_C = 128
G_C = 8
D_FF = 4096
KV_COLS = 2 * N_KV_B * HD_B
ATT_SCALE = HD_B ** -0.5
EPS = 1e-6
BIG = 1e6
NEG = -1e30
NEG_MASK = -3e38

AB_SEGS = ((0, 2048), (2048, 2560), (2560, 2816), (2816, 3072), (3072, 3328), (3328, 3456))
AB_COLS = 3352
AB_COLS_PAD = 3456

HGRN_CHUNK = 128
LANES = 128


def _cparams(sem, vmem_mb=None):
    kw = dict(dimension_semantics=sem)
    if vmem_mb is not None:
        kw["vmem_limit_bytes"] = vmem_mb << 20
    return pltpu.CompilerParams(**kw)


def _const_spec(shape):
    return pl.BlockSpec(shape, lambda *a: (0,) * len(shape))


def _dot(a, b):
    return jnp.dot(a, b, preferred_element_type=F32)


def _dot_nt(a, b):
    return lax.dot_general(a, b, (((1,), (1,)), ((), ())), preferred_element_type=F32)


def _split3(x):
    hi = x.astype(BF16)
    r = x - hi.astype(F32)
    mid = r.astype(BF16)
    lo = (r - mid.astype(F32)).astype(BF16)
    return hi, mid, lo


def _dot_01_lhs(m01, x):
    hi, mid, lo = _split3(x)
    return _dot(m01, hi) + _dot(m01, mid) + _dot(m01, lo)


def _dot_01_rhs(x, m01):
    hi, mid, lo = _split3(x)
    return _dot(hi, m01) + _dot(mid, m01) + _dot(lo, m01)


def _rms(x, g):
    return x * lax.rsqrt(jnp.mean(x * x, axis=-1, keepdims=True) + EPS) * g


def _gelu(x):
    return 0.5 * x * (1.0 + jnp.tanh(0.7978845608028654 * (x + 0.044715 * (x * x * x))))


def _sigmoid(x):
    return 1.0 / (1.0 + jnp.exp(-x))


def _silu(x):
    return x * _sigmoid(x)


def _softmax_masked(s, mask):
    sm = jnp.where(mask, s, NEG)
    m = jnp.max(sm, axis=-1, keepdims=True)
    e = jnp.where(mask, jnp.exp(sm - m), 0.0)
    return e / jnp.maximum(jnp.sum(e, axis=-1, keepdims=True), 1e-30)


def _in_proj_body(x_ref, g_ref, w_ref, *outs):
    xn = _rms(x_ref[...], g_ref[...]).astype(BF16)
    for (lo, hi), o_ref in zip(AB_SEGS, outs):
        o_ref[...] = _dot(xn, w_ref[:, lo:hi]).astype(o_ref.dtype)


def in_proj_ab(x, g, w):
    m = x.shape[0]
    tm = min(m, 512)
    widths = [hi - lo for lo, hi in AB_SEGS]
    dtypes = [F32, BF16, F32, F32, F32, F32]
    return pl.pallas_call(
        _in_proj_body,
        grid=(m // tm,),
        in_specs=[pl.BlockSpec((tm, D_MODEL), lambda i: (i, 0)),
                  _const_spec((1, D_MODEL)),
                  _const_spec((D_MODEL, AB_COLS_PAD))],
        out_specs=[pl.BlockSpec((tm, w_), lambda i: (i, 0)) for w_ in widths],
        out_shape=[jax.ShapeDtypeStruct((m, w_), dt) for w_, dt in zip(widths, dtypes)],
        compiler_params=_cparams(("parallel",), 56),
        name="in_proj_ab",
    )(x, g.reshape(1, D_MODEL), w)


def _in_proj_t_body(n_alias, x_ref, g_ref, w_ref, wkv_ref, *refs):
    za_ref, q_ref, gt_ref, cmp_ref, sel_ref, win_ref = refs[n_alias:]
    xn = _rms(x_ref[...], g_ref[...]).astype(BF16)
    za_ref[...] = _dot(xn, w_ref[:, 0:2048])
    q_ref[...] = _dot(xn, w_ref[:, 2048:2560]).astype(q_ref.dtype)
    gt_ref[...] = _dot(xn, w_ref[:, 2560:2688])
    kvt = _dot_nt(wkv_ref[...], xn)
    for i, o_ref in enumerate((cmp_ref, sel_ref, win_ref)):
        o_ref[...] = kvt[i * KV_COLS:(i + 1) * KV_COLS]


def in_proj_ab_t(x, g, w, wkv_t, j, n_ab, batch, prev):
    m = x.shape[0]
    t = m // batch
    tm = 512
    nt = t // tm
    n_alias = 0 if prev is None else 3
    row = lambda w_: pl.BlockSpec((tm, w_), lambda b, i: (b * nt + i, 0))
    kv_spec = pl.BlockSpec((None, None, KV_COLS, tm), lambda b, i: (j, b, 0, i))
    kv_shape = jax.ShapeDtypeStruct((n_ab, batch, KV_COLS, t), F32)
    return pl.pallas_call(
        functools.partial(_in_proj_t_body, n_alias),
        grid=(batch, nt),
        in_specs=[row(D_MODEL), _const_spec((1, D_MODEL)), _const_spec(w.shape), _const_spec(wkv_t.shape)]
        + [pl.BlockSpec(memory_space=pl.ANY)] * n_alias,
        out_specs=[row(2048), row(D_B), row(LANES), kv_spec, kv_spec, kv_spec],
        out_shape=[jax.ShapeDtypeStruct((m, 2048), F32), jax.ShapeDtypeStruct((m, D_B), BF16),
                   jax.ShapeDtypeStruct((m, LANES), F32), kv_shape, kv_shape, kv_shape],
        input_output_aliases={4 + i: 3 + i for i in range(n_alias)},
        compiler_params=_cparams(("parallel", "parallel"), 56),
        name="in_proj_ab_t",
    )(x, g.reshape(1, D_MODEL), w, wkv_t, *(() if prev is None else prev))


def _lower_bound(raw, j):
    m = jnp.max(raw, axis=0, keepdims=True)
    e = jnp.exp(raw - m)
    sm = e / jnp.sum(e, axis=0, keepdims=True)
    cs = sm[0:1]
    for i in range(1, j + 1):
        cs = cs + sm[i:i + 1]
    return cs - sm[0:1]


def _hgrn_consts(c):
    t = np.arange(c)[:, None]
    u = np.arange(c)[None, :]
    mats = [u <= t, u > t]
    masks = [t == u]
    for lvl in range(1, int(np.log2(c)) + 1):
        m = 1 << lvl
        half = m >> 1
        mid = (t // m) * m + half
        upper = (t % m) >= half
        mats.append((upper & (u >= mid) & (u <= t)) | (~upper & (u > t) & (u <= mid - 1)))
        masks.append(((t // m) == (u // m)) & upper & ((u % m) < half))
    return (np.concatenate(mats, 0).astype(np.float32), np.stack(masks).astype(np.float32))


def _hgrn_prompt_body(j, za_ref, lbraw_ref, hn_ref, mst_ref, lm_ref, o_ref, sfin_ref, st_ref):
    c = HGRN_CHUNK
    ci = pl.program_id(1)

    @pl.when(ci == 0)
    def _():
        st_ref[...] = jnp.zeros(st_ref.shape, F32)

    lb = _lower_bound(lbraw_ref[...], j)
    fg = lb + (1.0 - lb) * _sigmoid(za_ref[:, 512:1024])
    lf = jnp.log(fg)
    q = _silu(za_ref[:, 0:512])
    k = 1.0 - fg
    v = za_ref[:, 1024:1536]
    gg = _silu(za_ref[:, 1536:2048])
    nlev = lm_ref.shape[0]
    x_all = jnp.exp(_dot_01_lhs(mst_ref[...], lf))
    for h in range(H_A):
        sl = slice(h * DK_A, (h + 1) * DK_A)
        qh, kh, vh = q[:, sl], k[:, sl], v[:, sl]
        x = x_all[:, sl]
        att = lm_ref[0] * _dot_nt(qh.astype(BF16), kh.astype(BF16))
        for lvl in range(1, nlev):
            d = x[(1 + lvl) * c:(2 + lvl) * c]
            att = att + lm_ref[lvl] * _dot_nt((qh * d).astype(BF16), (kh * d).astype(BF16))
        st = st_ref[h]
        o = _dot_nt((qh * x[0:c]).astype(BF16), st.astype(BF16)) + _dot(att.astype(BF16), vh.astype(BF16))
        kout = (kh * x[c:2 * c]).astype(BF16)
        st_new = st * x[c - 1:c] + _dot(vh.T.astype(BF16), kout)
        st_ref[h] = st_new
        on = _rms(o, hn_ref[...]) * gg[:, sl]
        o_ref[:, sl] = on.astype(o_ref.dtype)

        @pl.when(ci == pl.num_programs(1) - 1)
        def _():
            sfin_ref[h] = st_new.T


def hgrn_prompt(za, lb_raw, hnorm, j, batch):
    m = za.shape[0]
    t = m // batch
    c = HGRN_CHUNK
    nchunk = t // c
    mst, lm = _hgrn_consts(c)
    return pl.pallas_call(
        functools.partial(_hgrn_prompt_body, j),
        grid=(batch, nchunk),
        in_specs=[pl.BlockSpec((c, 2048), lambda b, i: (b * nchunk + i, 0)),
                  _const_spec(lb_raw.shape),
                  _const_spec((1, DK_A)),
                  _const_spec(mst.shape),
                  _const_spec(lm.shape)],
        out_specs=[pl.BlockSpec((c, D_A), lambda b, i: (b * nchunk + i, 0)),
                   pl.BlockSpec((None, H_A, DK_A, DK_A), lambda b, i: (b, 0, 0, 0))],
        out_shape=[jax.ShapeDtypeStruct((m, D_A), BF16),
                   jax.ShapeDtypeStruct((batch, H_A, DK_A, DK_A), F32)],
        scratch_shapes=[pltpu.VMEM((H_A, DK_A, DK_A), F32)],
        compiler_params=_cparams(("parallel", "arbitrary"), 48),
        name="hgrn_prompt",
    )(za, lb_raw, hnorm.reshape(1, DK_A), jnp.asarray(mst, BF16), jnp.asarray(lm, F32))


def _hgrn_step_body(j, bt, za_ref, s_ref, lbraw_ref, hn_ref, o_ref, snew_ref):
    lb = _lower_bound(lbraw_ref[...], j)
    fg = lb + (1.0 - lb) * _sigmoid(za_ref[:, 512:1024])
    q = _silu(za_ref[:, 0:512])
    k = 1.0 - fg
    v = za_ref[:, 1024:1536]
    gg = _silu(za_ref[:, 1536:2048])
    zpad = jnp.zeros((LANES - bt, DK_A), F32)

    def cols(a):
        return jnp.concatenate([a, zpad], axis=0).T

    for h in range(H_A):
        sl = slice(h * DK_A, (h + 1) * DK_A)
        qc, fc, kc = cols(q[:, sl]), cols(fg[:, sl]), cols(k[:, sl])
        for b in range(bt):
            s_new = fc[:, b:b + 1] * s_ref[b, h] + kc[:, b:b + 1] * v[b:b + 1, sl]
            snew_ref[b, h] = s_new
            o = jnp.sum(qc[:, b:b + 1] * s_new, axis=0, keepdims=True)
            o_ref[b:b + 1, sl] = (_rms(o, hn_ref[...]) * gg[b:b + 1, sl]).astype(o_ref.dtype)


def hgrn_step(za, state, lb_raw, hnorm, j):
    nb = za.shape[0]
    bt = 8
    return pl.pallas_call(
        functools.partial(_hgrn_step_body, j, bt),
        grid=(nb // bt,),
        in_specs=[pl.BlockSpec((bt, 2048), lambda i: (i, 0)),
                  pl.BlockSpec((None, bt, H_A, DK_A, DK_A), lambda i: (j, i, 0, 0, 0)),
                  _const_spec(lb_raw.shape),
                  _const_spec((1, DK_A))],
        out_specs=[pl.BlockSpec((bt, D_A), lambda i: (i, 0)),
                   pl.BlockSpec((bt, H_A, DK_A, DK_A), lambda i: (i, 0, 0, 0))],
        out_shape=[jax.ShapeDtypeStruct((nb, D_A), BF16),
                   jax.ShapeDtypeStruct(state.shape[1:], F32)],
        compiler_params=_cparams(("parallel",)),
        name="hgrn_step",
    )(za, state, lb_raw, hnorm.reshape(1, DK_A))


def _compress_weights(pe, w1, w2):
    eye = jnp.eye(N_KV_B, dtype=F32)
    w1r = w1.reshape(2, CMP_BLOCK, HD_B, CMP_HIDDEN)
    w1x = jnp.einsum("krdh,gG->rkgdGh", w1r, eye).reshape(CMP_BLOCK, 2, N_KV_B * HD_B, N_KV_B * CMP_HIDDEN)
    pex = jnp.broadcast_to(pe.transpose(1, 0, 2)[:, :, None, :], (CMP_BLOCK, 2, N_KV_B, HD_B))
    pex = pex.reshape(CMP_BLOCK, 2, 1, N_KV_B * HD_B)
    w2xt = jnp.einsum("khd,gG->kGdgh", w2, eye).reshape(2, N_KV_B * HD_B, N_KV_B * CMP_HIDDEN)
    return w1x.astype(BF16), pex.astype(F32), w2xt.astype(BF16)


def _compress_body(npp, nseg, *refs):
    pages = refs[1:1 + npp]
    w1_ref, pe_ref, w2_ref, o_ref, xk, xv, bbuf = refs[1 + npp:]
    s = pl.program_id(1)
    for i, pg in enumerate(pages):
        row0 = pl.multiple_of((s * npp + i) * PAGE_SIZE, PAGE_SIZE)
        xk[pl.ds(row0, PAGE_SIZE), :] = pg[0].T
        xv[pl.ds(row0, PAGE_SIZE), :] = pg[1].T

    @pl.when(s == pl.num_programs(1) - 1)
    def _():
        bbuf[pl.ds(nseg, 8), :] = jnp.zeros((8, N_KV_B * CMP_HIDDEN), F32)
        for kv, xbuf in enumerate((xk, xv)):
            acc_a = jnp.zeros((nseg, N_KV_B * CMP_HIDDEN), F32)
            acc_b = jnp.zeros((nseg, N_KV_B * CMP_HIDDEN), F32)
            for r in range(CMP_STRIDE):
                xr = xbuf[pl.ds(r, nseg, stride=CMP_STRIDE), :]
                acc_a = acc_a + _dot((xr + pe_ref[r, kv]).astype(BF16), w1_ref[r, kv])
                acc_b = acc_b + _dot((xr + pe_ref[CMP_STRIDE + r, kv]).astype(BF16), w1_ref[CMP_STRIDE + r, kv])
            bbuf[pl.ds(0, nseg), :] = acc_b
            hid = _gelu(acc_a + bbuf[pl.ds(1, nseg), :])
            o_ref[kv] = _dot_nt(w2_ref[kv], hid.astype(BF16))


def compress_pages(table, pages_t, j, lane_paged, w1x, pex, w2xt, npp):
    nb, npg = table.shape
    nseg = npg * PAGE_SIZE // CMP_STRIDE
    half = KV_COLS // 2
    if lane_paged:
        imap = lambda b, s, t, i: (j, b, 0, 0, t[jnp.minimum(b, nb - 1), s * npp + i])
    else:
        imap = lambda b, s, t, i: (j, t[jnp.minimum(b, nb - 1), s * npp + i], 0, 0, 0)
    page_specs = [pl.BlockSpec((None, None, 2, half, PAGE_SIZE), functools.partial(imap, i=i)) for i in range(npp)]
    grid_spec = pltpu.PrefetchScalarGridSpec(
        num_scalar_prefetch=1,
        grid=(nb, npg // npp),
        in_specs=page_specs + [_const_spec(w1x.shape), _const_spec(pex.shape), _const_spec(w2xt.shape)],
        out_specs=pl.BlockSpec((None, 2, half, nseg), lambda b, s, t: (b, 0, 0, 0)),
        scratch_shapes=[pltpu.VMEM((npg * PAGE_SIZE, half), F32),
                        pltpu.VMEM((npg * PAGE_SIZE, half), F32),
                        pltpu.VMEM((nseg + 8, N_KV_B * CMP_HIDDEN), F32)],
    )
    return pl.pallas_call(
        functools.partial(_compress_body, npp, nseg),
        grid_spec=grid_spec,
        out_shape=jax.ShapeDtypeStruct((nb, 2, half, nseg), F32),
        compiler_params=_cparams(("parallel", "arbitrary"), 56),
        name="compress_pages",
    )(table, *([pages_t] * npp), w1x, pex, w2xt)


def _overlap_matrix(n_cmp_pad, n_sel_pad, n_cmp, n_sel):
    cs = np.arange(n_cmp_pad)[:, None] * CMP_STRIDE
    ss = np.arange(n_sel_pad)[None, :] * SEL_BLOCK
    ov = (cs < ss + SEL_BLOCK) & (cs + CMP_BLOCK > ss)
    ov &= (np.arange(n_cmp_pad)[:, None] < n_cmp) & (np.arange(n_sel_pad)[None, :] < n_sel)
    return ov.astype(np.float32)


def _rank(imp, jidx, cur):
    valid = jidx <= cur
    dist = cur - jidx
    forced = (jidx == 0) | ((dist >= 0) & (dist < N_LOCAL))
    return jnp.where(valid, imp + BIG * forced.astype(F32), -BIG)


NSA_TQ = 512
NSA_TK = 512
NSA_TW = 256
NSA_RB = 64
NSA_WB = 256


def _nsa_prompt_body(n_cmp, n_sel, q_ref, kct_ref, vct_ref, kst_ref, vst_ref, kwt_ref, vwt_ref, gt_ref, ov_ref,
                     eb_ref, o_ref, kc2, vc2e, vc2o, k2e, k2o, v2e, v2o, kw2, vw2e, vw2o, s_sc, e_sc, m_sc, acc_sc,
                     sw_sc, ew_sc, cnt_sc):
    tq, tk, tw, rb_ = NSA_TQ, NSA_TK, NSA_TW, NSA_RB
    hd = HD_B
    g = pl.program_id(1)
    qi = pl.program_id(2)
    t0 = qi * tq
    t_len = kst_ref.shape[1]
    ncp = kct_ref.shape[1]

    @pl.when(qi == 0)
    def _():
        zc = jnp.zeros((hd, ncp), BF16)
        kct = kct_ref[...].astype(BF16)
        vct = vct_ref[...].astype(BF16)
        kc2[0:hd] = kct
        kc2[hd:2 * hd] = kct
        vc2e[0:hd] = vct
        vc2e[hd:2 * hd] = zc
        vc2o[0:hd] = zc
        vc2o[hd:2 * hd] = vct
        ones_row = (lax.broadcasted_iota(I32, (hd, tk), 0) == 0).astype(BF16)
        for kt in range(t_len // tk):
            cols = slice(kt * tk, (kt + 1) * tk)
            kt_ = kst_ref[:, cols].astype(BF16)
            vt_ = vst_ref[:, cols].astype(BF16)
            k2e[kt, 0:hd] = kt_
            k2e[kt, hd:2 * hd] = eb_ref[kt]
            k2o[kt, 0:hd] = eb_ref[kt]
            k2o[kt, hd:2 * hd] = kt_
            v2e[kt, 0:hd] = vt_
            v2e[kt, hd:2 * hd] = ones_row
            v2o[kt, 0:hd] = ones_row
            v2o[kt, hd:2 * hd] = vt_
        zw = (lax.broadcasted_iota(I32, (hd, tw), 0) == 0).astype(BF16)
        for wt in range(t_len // tw):
            cols = slice(wt * tw, (wt + 1) * tw)
            kt_ = kwt_ref[:, cols].astype(BF16)
            vt_ = vwt_ref[:, cols].astype(BF16)
            kw2[wt, 0:hd] = kt_
            kw2[wt, hd:2 * hd] = kt_
            vw2e[wt, 0:hd] = vt_
            vw2e[wt, hd:2 * hd] = zw
            vw2o[wt, 0:hd] = zw
            vw2o[wt, hd:2 * hd] = vt_

    lane = lax.broadcasted_iota(I32, (1, LANES), 1)
    even = lane < hd
    tpos = t0 + lax.broadcasted_iota(I32, (tq, 1), 0)
    gts = _sigmoid(gt_ref[...])
    gsel = jnp.where(g == 0, gts[:, 0:REP_B * 3], gts[:, REP_B * 3:2 * REP_B * 3])
    gate = lambda r, c: gsel[:, 3 * r + c:3 * r + c + 1]
    zq = jnp.zeros((tq, LANES), q_ref.dtype)
    q128 = [q_ref[:, p * LANES:(p + 1) * LANES] * ATT_SCALE for p in range(REP_B // 2)]
    qh = [jnp.where(even if r % 2 == 0 else ~even, q128[r // 2], zq) for r in range(REP_B)]

    n_idx = lax.broadcasted_iota(I32, (1, ncp), 1)
    cmask = ((n_idx * CMP_STRIDE + (CMP_BLOCK - 1)) <= tpos) & (n_idx < n_cmp)
    o_cmp = []
    psum = jnp.zeros((tq, ncp), F32)
    for r in range(REP_B):
        p = _softmax_masked(_dot(qh[r], kc2[...]), cmask)
        psum = psum + p
        o_cmp.append(_dot_nt(p.astype(BF16), (vc2e if r % 2 == 0 else vc2o)[...]))
    imp = _dot_01_rhs(psum, ov_ref[...])

    imp_t = imp.T[0:n_sel]
    jidx = lax.broadcasted_iota(I32, (n_sel, tq), 0)
    cur = jnp.right_shift(t0 + lax.broadcasted_iota(I32, (n_sel, tq), 1), 6)
    rank = _rank(imp_t, jidx, cur)
    cnt_sc[...] = jnp.zeros(cnt_sc.shape, I32)
    grp_n = 8
    for grp in range(n_sel // grp_n):
        @pl.when(grp * grp_n * SEL_BLOCK < t0 + tq)
        def _():
            c = cnt_sc[...]
            for i in range(grp * grp_n, (grp + 1) * grp_n):
                ri = rank[i:i + 1]
                c = c + ((ri > rank) | ((ri == rank) & (jidx > i))).astype(I32)
            cnt_sc[...] = c
    cnt = cnt_sc[...]
    drop_t = (cnt >= min(N_SEL, n_sel)).astype(F32)
    if n_sel < hd:
        drop_t = jnp.concatenate([drop_t, jnp.zeros((hd - n_sel, tq), F32)], axis=0)
    drop = jnp.concatenate([drop_t, drop_t], axis=0).T.astype(q_ref.dtype)
    q2 = [jnp.where(even if r % 2 == 0 else ~even, q128[r // 2], drop) for r in range(REP_B)]

    m_sc[...] = jnp.full(m_sc.shape, NEG_MASK, F32)
    acc_sc[...] = jnp.zeros(acc_sc.shape, F32)

    def kv_tile(kt, causal):
        k0 = kt * tk
        kpos = k0 + lax.broadcasted_iota(I32, (1, tk), 1)
        for r in range(REP_B):
            s_sc[r] = _dot(q2[r], (k2e if r % 2 == 0 else k2o)[kt])
        for r in range(REP_B):
            for rb in range(tq // rb_):
                rows = slice(rb * rb_, (rb + 1) * rb_)
                sb = s_sc[r, rows]
                if causal:
                    sb = jnp.where(kpos <= tpos[rows], sb, NEG_MASK)
                m_old = m_sc[r, rows]
                m_new = jnp.maximum(m_old, jnp.max(sb, axis=-1, keepdims=True))
                m_sc[r, rows] = m_new
                e_sc[r, rows] = jnp.exp(sb - m_new).astype(e_sc.dtype)
                acc_sc[r, rows] = jnp.exp(m_old - m_new) * acc_sc[r, rows]
        for r in range(REP_B):
            acc_sc[r] += _dot_nt(e_sc[r], (v2e if r % 2 == 0 else v2o)[kt])

    n_full = t0 // tk

    def full_tile(kt, carry):
        kv_tile(kt, False)
        return carry

    lax.fori_loop(0, n_full, full_tile, 0)
    kv_tile(n_full, True)
    def normalised(a, r):
        own = even if r % 2 == 0 else ~even
        l = a[:, hd:hd + 1] if r % 2 == 0 else a[:, 0:1]
        return jnp.where(own, a / jnp.maximum(l, 1e-30), 0.0)

    o_sel = [normalised(acc_sc[r], r) for r in range(REP_B)]

    wb = NSA_WB
    nwt = (WINDOW + wb) // tw
    o_win = [[] for _ in range(REP_B)]
    for hb in range(tq // wb):
        rows = slice(hb * wb, (hb + 1) * wb)
        w0 = jnp.maximum(t0 + hb * wb - WINDOW, 0)
        wt0 = w0 // tw
        wpos = w0 + lax.broadcasted_iota(I32, (1, nwt * tw), 1)
        for r in range(REP_B):
            for c in range(nwt):
                sw_sc[r, c] = _dot(qh[r][rows], kw2[wt0 + c])
        for r in range(REP_B):
            for rb in range(wb // rb_):
                rr = slice(rb * rb_, (rb + 1) * rb_)
                dist = tpos[hb * wb + rb * rb_:hb * wb + (rb + 1) * rb_] - wpos
                wmask = (dist >= 0) & (dist < WINDOW)
                sb = jnp.where(wmask, jnp.concatenate([sw_sc[r, c, rr] for c in range(nwt)], axis=1), NEG)
                e = jnp.exp(sb - jnp.max(sb, axis=-1, keepdims=True)).astype(ew_sc.dtype)
                for c in range(nwt):
                    ew_sc[r, c, rr] = e[:, c * tw:(c + 1) * tw]
        for r in range(REP_B):
            vw = vw2e if r % 2 == 0 else vw2o
            ow = _dot_nt(ew_sc[r, 0], vw[wt0])
            for c in range(1, nwt):
                ow = ow + _dot_nt(ew_sc[r, c], vw[wt0 + c])
            o_win[r].append(normalised(ow, r))
    o_win = [jnp.concatenate(parts, axis=0) for parts in o_win]

    for p in range(REP_B // 2):
        acc = jnp.zeros((tq, LANES), F32)
        for r in (2 * p, 2 * p + 1):
            acc = acc + gate(r, 0) * o_cmp[r] + gate(r, 1) * o_sel[r] + gate(r, 2) * o_win[r]
        o_ref[:, p * LANES:(p + 1) * LANES] = acc.astype(o_ref.dtype)


def nsa_prompt(q, kct, kvt_sel, kvt_win, gates, j, batch, n_cmp):
    m = q.shape[0]
    t = m // batch
    tq, tk, tw = NSA_TQ, NSA_TK, NSA_TW
    hd = HD_B
    n_sel = t // SEL_BLOCK
    ncp = kct.shape[3]
    nqt = t // tq
    ov = jnp.asarray(_overlap_matrix(ncp, LANES, n_cmp, n_sel), BF16)
    kb = np.arange(t) // SEL_BLOCK
    eb = np.where(np.arange(hd)[:, None] == kb[None, :], NEG_MASK, 0.0).astype(np.float32)
    eb = jnp.asarray(eb.reshape(hd, t // tk, tk).transpose(1, 0, 2), BF16)
    n_ab = kvt_sel.shape[0]
    kct4 = kct.reshape(batch, 2 * N_KV_B, hd, ncp)
    sel5 = kvt_sel.reshape(n_ab, batch, 2 * N_KV_B, hd, t)
    win5 = kvt_win.reshape(n_ab, batch, 2 * N_KV_B, hd, t)
    cspec = lambda kv: pl.BlockSpec((None, None, hd, ncp), lambda b, g, i: (b, kv * N_KV_B + g, 0, 0))
    rspec = lambda kv: pl.BlockSpec((None, None, None, hd, t), lambda b, g, i: (j, b, kv * N_KV_B + g, 0, 0))
    slab = lambda n, w: pltpu.VMEM((n, 2 * hd, w), BF16)
    return pl.pallas_call(
        functools.partial(_nsa_prompt_body, n_cmp, n_sel),
        grid=(batch, N_KV_B, nqt),
        in_specs=[pl.BlockSpec((tq, REP_B * hd), lambda b, g, i: (b * nqt + i, g)),
                  cspec(0), cspec(1), rspec(0), rspec(1), rspec(0), rspec(1),
                  pl.BlockSpec((tq, LANES), lambda b, g, i: (b * nqt + i, 0)),
                  _const_spec(ov.shape), _const_spec(eb.shape)],
        out_specs=pl.BlockSpec((tq, REP_B * hd), lambda b, g, i: (b * nqt + i, g)),
        out_shape=jax.ShapeDtypeStruct((m, N_KV_B * REP_B * hd), BF16),
        scratch_shapes=[pltpu.VMEM((2 * hd, ncp), BF16), pltpu.VMEM((2 * hd, ncp), BF16), pltpu.VMEM((2 * hd, ncp), BF16),
                        slab(t // tk, tk), slab(t // tk, tk), slab(t // tk, tk), slab(t // tk, tk),
                        slab(t // tw, tw), slab(t // tw, tw), slab(t // tw, tw),
                        pltpu.VMEM((REP_B, tq, tk), F32), pltpu.VMEM((REP_B, tq, tk), BF16),
                        pltpu.VMEM((REP_B, tq, 1), F32), pltpu.VMEM((REP_B, tq, LANES), F32),
                        pltpu.VMEM((REP_B, (WINDOW + NSA_WB) // tw, NSA_WB, tw), F32),
                        pltpu.VMEM((REP_B, (WINDOW + NSA_WB) // tw, NSA_WB, tw), BF16),
                        pltpu.VMEM((n_sel, tq), I32)],
        compiler_params=_cparams(("parallel", "parallel", "arbitrary"), 56),
        name="nsa_prompt",
    )(q, kct4, kct4, sel5, sel5, win5, win5, gates, ov, eb)


def _group_queries(q8, g):
    z = jnp.zeros_like(q8)
    return jnp.concatenate([q8, z] if g == 0 else [z, q8], axis=1)


def _nsa_step_select_body(q_pos, n_cmp, n_sel, q_ref, kct_ref, ov_ref, ocmp_ref, idx_ref):
    q8 = q_ref[...] * ATT_SCALE
    ncp = kct_ref.shape[2]
    nsp = ov_ref.shape[1]
    n_idx = lax.broadcasted_iota(I32, (1, ncp), 1)
    cmask = ((n_idx * CMP_STRIDE + (CMP_BLOCK - 1)) <= q_pos) & (n_idx < n_cmp)
    head = lax.broadcasted_iota(I32, (N_KV_B * REP_B, 1), 0)
    kct = kct_ref[0].astype(BF16)
    vct = kct_ref[1].astype(BF16)
    o_all = jnp.zeros((N_KV_B * REP_B, HD_B), F32)
    for g in range(N_KV_B):
        ingrp = (head // REP_B) == g
        p = _softmax_masked(_dot(_group_queries(q8, g), kct), cmask)
        o_g = _dot_nt(p.astype(BF16), vct)
        o_all = jnp.where(ingrp, o_g[:, g * HD_B:(g + 1) * HD_B], o_all)
        imp8 = _dot_01_rhs(jnp.where(ingrp, p, 0.0), ov_ref[...])
        imp = jnp.sum(imp8, axis=0, keepdims=True)
        lane_j = lax.broadcasted_iota(I32, (nsp, nsp), 1)
        sub_i = lax.broadcasted_iota(I32, (nsp, nsp), 0)
        cur = q_pos // SEL_BLOCK
        rank_row = jnp.broadcast_to(_rank(imp, lane_j[0:1], cur), (nsp, nsp))
        rank_col = rank_row.T
        before = (rank_row > rank_col) | ((rank_row == rank_col) & (lane_j < sub_i))
        order = jnp.sum(before.astype(F32), axis=1, keepdims=True)
        slot = lax.broadcasted_iota(I32, (nsp, LANES), 1).astype(F32)
        hit = (order == slot) & (sub_i[:, 0:1] < n_sel)
        blk = lax.broadcasted_iota(I32, (nsp, LANES), 0).astype(F32)
        idx = jnp.sum(jnp.where(hit, blk, 0.0), axis=0, keepdims=True)
        idx_ref[g:g + 1, :] = idx.astype(I32)
    ocmp_ref[...] = o_all


def nsa_step_select(q, kct, n_cmp, n_sel, q_pos):
    nb = q.shape[0]
    ncp = kct.shape[3]
    nsp = 256
    nh = N_KV_B * REP_B
    ov = jnp.asarray(_overlap_matrix(ncp, nsp, n_cmp, n_sel), BF16)
    return pl.pallas_call(
        functools.partial(_nsa_step_select_body, q_pos, n_cmp, n_sel),
        grid=(nb,),
        in_specs=[pl.BlockSpec((None, nh, HD_B), lambda b: (b, 0, 0)),
                  pl.BlockSpec((None, 2, KV_COLS // 2, ncp), lambda b: (b, 0, 0, 0)),
                  _const_spec(ov.shape)],
        out_specs=[pl.BlockSpec((None, nh, HD_B), lambda b: (b, 0, 0)),
                   pl.BlockSpec((None, N_KV_B, LANES), lambda b: (b, 0, 0))],
        out_shape=[jax.ShapeDtypeStruct((nb, nh, HD_B), F32),
                   jax.ShapeDtypeStruct((nb, N_KV_B, LANES), I32)],
        compiler_params=_cparams(("parallel",)),
        name="nsa_step_select",
    )(q, kct, ov)


def _attend_with_new(qp, kt, vt, kmask, k_new, v_new, new_ok):
    s = _dot(qp, kt)
    s_new = jnp.sum(qp.astype(F32) * k_new, axis=1, keepdims=True)
    sm = jnp.where(kmask, s, NEG)
    sn = jnp.where(new_ok, s_new, NEG)
    m = jnp.maximum(jnp.max(sm, axis=-1, keepdims=True), sn)
    e = jnp.where(kmask, jnp.exp(sm - m), 0.0)
    e_new = jnp.where(new_ok, jnp.exp(sn - m), 0.0)
    l = jnp.sum(e, axis=-1, keepdims=True) + e_new
    return (_dot_nt(e.astype(BF16), vt) + e_new * v_new) / jnp.maximum(l, 1e-30)


def _nsa_step_attend_body(q_pos, past_len, k_top, *refs):
    tbl_ref, idx_ref = refs[0], refs[1]
    pages = refs[2:2 + N_KV_B * k_top]
    (q_ref, ksn_ref, win_ref, kwn_ref, ocmp_ref, gt_ref, o_ref, kt_sc, vt_sc) = refs[2 + N_KV_B * k_top:]
    b = pl.program_id(0)
    nh = N_KV_B * REP_B
    half = KV_COLS // 2
    q8 = q_ref[...] * ATT_SCALE
    head = lax.broadcasted_iota(I32, (nh, 1), 0)
    nkeys = k_top * PAGE_SIZE
    wlen = win_ref.shape[2]
    wrow = lax.broadcasted_iota(I32, (1, wlen), 1)
    wdist = q_pos - (past_len - wlen + wrow)
    wmask = (wdist >= 0) & (wdist < WINDOW)
    kwt = win_ref[0].astype(BF16)
    vwt = win_ref[1].astype(BF16)
    ks_new, vs_new = ksn_ref[:, 0:half], ksn_ref[:, half:KV_COLS]
    kw_new, vw_new = kwn_ref[:, 0:half], kwn_ref[:, half:KV_COLS]
    o_sel = jnp.zeros((nh, HD_B), F32)
    o_win = jnp.zeros((nh, HD_B), F32)
    for g in range(N_KV_B):
        ingrp = (head // REP_B) == g
        qp = _group_queries(q8, g)
        has_new = jnp.zeros((), I32)
        krow = lax.broadcasted_iota(I32, (1, nkeys), 1)
        kmask = jnp.zeros((1, nkeys), jnp.bool_)
        for i in range(k_top):
            pg = pages[g * k_top + i]
            kt_sc[:, i * PAGE_SIZE:(i + 1) * PAGE_SIZE] = pg[0].astype(BF16)
            vt_sc[:, i * PAGE_SIZE:(i + 1) * PAGE_SIZE] = pg[1].astype(BF16)
            jb = idx_ref[b, g, i]
            kpos = (jb * SEL_BLOCK // PAGE_SIZE) * PAGE_SIZE + (krow - i * PAGE_SIZE)
            inblk = (krow >= i * PAGE_SIZE) & (krow < (i + 1) * PAGE_SIZE) & (kpos // SEL_BLOCK == jb)
            kmask = kmask | (inblk & (kpos <= q_pos) & (kpos < past_len))
            has_new = has_new | (jb * SEL_BLOCK + SEL_BLOCK > past_len).astype(I32)
        os_ = _attend_with_new(qp, kt_sc[...], vt_sc[...], kmask, ks_new, vs_new, has_new > 0)
        o_sel = jnp.where(ingrp, os_[:, g * HD_B:(g + 1) * HD_B], o_sel)
        ow_ = _attend_with_new(qp, kwt, vwt, wmask, kw_new, vw_new, True)
        o_win = jnp.where(ingrp, ow_[:, g * HD_B:(g + 1) * HD_B], o_win)
    gt = _sigmoid(gt_ref[...])
    o_ref[...] = (gt[:, 0:1] * ocmp_ref[...] + gt[:, 1:2] * o_sel + gt[:, 2:3] * o_win).astype(o_ref.dtype)


def nsa_step_attend(table, idx, cache_sel_t, win_t, j, q, kvs_new, kvw_new, o_cmp, gates, q_pos, past_len):
    nb = q.shape[0]
    k_top = N_SEL
    wlen = win_t.shape[4]
    half = KV_COLS // 2
    per_page = PAGE_SIZE // SEL_BLOCK
    last_cached = past_len // SEL_BLOCK - 1

    def page_map(b, t, ix, g, i):
        bb = jnp.minimum(b, nb - 1)
        jb = jnp.minimum(ix[bb, g, i], last_cached)
        return (j, t[bb, jb // per_page], 0, 0, 0)

    page_specs = [pl.BlockSpec((None, None, 2, half, PAGE_SIZE), functools.partial(page_map, g=g, i=i))
                  for g in range(N_KV_B) for i in range(k_top)]
    nh = N_KV_B * REP_B
    row_spec = lambda w: pl.BlockSpec((None, nh, w), lambda b, t, ix: (b, 0, 0))
    new_spec = pl.BlockSpec((None, 1, KV_COLS), lambda b, t, ix: (b, 0, 0))
    grid_spec = pltpu.PrefetchScalarGridSpec(
        num_scalar_prefetch=2,
        grid=(nb,),
        in_specs=page_specs + [row_spec(HD_B), new_spec,
                               pl.BlockSpec((None, None, 2, half, wlen), lambda b, t, ix: (j, b, 0, 0, 0)),
                               new_spec, row_spec(HD_B), row_spec(3)],
        out_specs=row_spec(HD_B),
        scratch_shapes=[pltpu.VMEM((half, k_top * PAGE_SIZE), BF16),
                        pltpu.VMEM((half, k_top * PAGE_SIZE), BF16)],
    )
    return pl.pallas_call(
        functools.partial(_nsa_step_attend_body, q_pos, past_len, k_top),
        grid_spec=grid_spec,
        out_shape=jax.ShapeDtypeStruct((nb, nh, HD_B), BF16),
        compiler_params=_cparams(("arbitrary",)),
        name="nsa_step_attend",
    )(table, idx, *([cache_sel_t] * (N_KV_B * k_top)), q, kvs_new, win_t, kvw_new, o_cmp, gates)


def _out_proj_body(oa_ref, ob_ref, w_ref, r_ref, o_ref):
    o_ref[...] = r_ref[...] + _dot(oa_ref[...], w_ref[0:D_A, :]) + _dot(ob_ref[...], w_ref[D_A:D_A + D_B, :])


def out_proj(o_a, o_b, w, resid):
    m = resid.shape[0]
    tm = min(m, 512)
    return pl.pallas_call(
        _out_proj_body,
        grid=(m // tm,),
        in_specs=[pl.BlockSpec((tm, D_A), lambda i: (i, 0)), pl.BlockSpec((tm, D_B), lambda i: (i, 0)),
                  _const_spec(w.shape), pl.BlockSpec((tm, D_MODEL), lambda i: (i, 0))],
        out_specs=pl.BlockSpec((tm, D_MODEL), lambda i: (i, 0)),
        out_shape=jax.ShapeDtypeStruct((m, D_MODEL), F32),
        compiler_params=_cparams(("parallel",)),
        name="out_proj",
    )(o_a, o_b, w, resid)


def _ffn_body(final, x_ref, g_ref, w1_ref, w2_ref, gf_ref, o_ref, xn_sc, acc_sc):
    jf = pl.program_id(1)

    @pl.when(jf == 0)
    def _():
        xn_sc[...] = _rms(x_ref[...], g_ref[...]).astype(BF16)
        acc_sc[...] = jnp.zeros(acc_sc.shape, F32)

    h = jnp.maximum(_dot(xn_sc[...], w1_ref[...]), 0.0)
    acc_sc[...] += _dot((h * h).astype(BF16), w2_ref[...])

    @pl.when(jf == pl.num_programs(1) - 1)
    def _():
        y = x_ref[...] + acc_sc[...]
        o_ref[...] = _rms(y, gf_ref[...]) if final else y


def ffn(x, g, w1, w2, g_final, final):
    m = x.shape[0]
    tm = min(m, 1024)
    tf = 512
    return pl.pallas_call(
        functools.partial(_ffn_body, final),
        grid=(m // tm, D_FF // tf),
        in_specs=[pl.BlockSpec((tm, D_MODEL), lambda i, j: (i, 0)),
                  _const_spec((1, D_MODEL)),
                  pl.BlockSpec((D_MODEL, tf), lambda i, j: (0, j)),
                  pl.BlockSpec((tf, D_MODEL), lambda i, j: (j, 0)),
                  _const_spec((1, D_MODEL))],
        out_specs=pl.BlockSpec((tm, D_MODEL), lambda i, j: (i, 0)),
        out_shape=jax.ShapeDtypeStruct((m, D_MODEL), F32),
        scratch_shapes=[pltpu.VMEM((tm, D_MODEL), BF16), pltpu.VMEM((tm, D_MODEL), F32)],
        compiler_params=_cparams(("parallel", "arbitrary"), 48),
        name="ffn",
    )(x, g.reshape(1, D_MODEL), w1, w2, g_final.reshape(1, D_MODEL))


def _layernorm(v, g, b):
    mu = jnp.mean(v, axis=-1, keepdims=True)
    var = jnp.mean(jnp.square(v - mu), axis=-1, keepdims=True)
    return (v - mu) * lax.rsqrt(var + EPS) * g + b


def _gmlp_prompt_body(tm, x_ref, g_ref, win_ref, lng_ref, lnb_ref, ws_ref, bs_ref, wout_ref, o_ref, mix_sc):
    x = x_ref[...]
    z = _gelu(_dot(_rms(x, g_ref[...]).astype(BF16), win_ref[...]))
    u = z[:, 0:D_MODEL]
    v = _layernorm(z[:, D_MODEL:2 * D_MODEL], lng_ref[...], lnb_ref[...]).astype(BF16)
    row = lax.broadcasted_iota(I32, (CHUNK_C, CHUNK_C), 0)
    col = lax.broadcasted_iota(I32, (CHUNK_C, CHUNK_C), 1)
    gcd = D_MODEL // G_C
    for gi in range(G_C):
        ws = jnp.where(row >= col, ws_ref[gi], 0.0).astype(BF16)
        for ch in range(tm // CHUNK_C):
            rows = slice(ch * CHUNK_C, (ch + 1) * CHUNK_C)
            cols = slice(gi * gcd, (gi + 1) * gcd)
            mix_sc[rows, cols] = _dot(ws, v[rows, cols]) + bs_ref[gi]
    o_ref[...] = x + _dot((u * mix_sc[...]).astype(BF16), wout_ref[...])


def gmlp_prompt(x, g, w_in, ln_g, ln_b, w_s, b_s, w_out):
    m = x.shape[0]
    tm = 256
    return pl.pallas_call(
        functools.partial(_gmlp_prompt_body, tm),
        grid=(m // tm,),
        in_specs=[pl.BlockSpec((tm, D_MODEL), lambda i: (i, 0)),
                  _const_spec((1, D_MODEL)), _const_spec(w_in.shape),
                  _const_spec((1, D_MODEL)), _const_spec((1, D_MODEL)),
                  _const_spec(w_s.shape), _const_spec((G_C, CHUNK_C, 1)), _const_spec(w_out.shape)],
        out_specs=pl.BlockSpec((tm, D_MODEL), lambda i: (i, 0)),
        out_shape=jax.ShapeDtypeStruct((m, D_MODEL), F32),
        scratch_shapes=[pltpu.VMEM((tm, D_MODEL), F32)],
        compiler_params=_cparams(("parallel",), 48),
        name="gmlp_prompt",
    )(x, g.reshape(1, D_MODEL), w_in, ln_g.reshape(1, D_MODEL), ln_b.reshape(1, D_MODEL),
      w_s, b_s.reshape(G_C, CHUNK_C, 1), w_out)


def _gmlp_step_body(x_ref, g_ref, win_ref, lng_ref, lnb_ref, sc_ref, bi_ref, wout_ref, o_ref, v_ref):
    x = x_ref[...]
    z = _gelu(_dot(_rms(x, g_ref[...]).astype(BF16), win_ref[...]))
    v = _layernorm(z[:, D_MODEL:2 * D_MODEL], lng_ref[...], lnb_ref[...])
    v_ref[...] = v
    mix = v * sc_ref[...] + bi_ref[...]
    o_ref[...] = x + _dot((z[:, 0:D_MODEL] * mix).astype(BF16), wout_ref[...])


def gmlp_step(x, g, w_in, ln_g, ln_b, w_s, b_s, w_out):
    m = x.shape[0]
    gcd = D_MODEL // G_C
    scale = jnp.repeat(w_s[:, 0, 0], gcd).reshape(1, D_MODEL)
    bias = jnp.repeat(b_s[:, 0], gcd).reshape(1, D_MODEL)
    row = _const_spec((1, D_MODEL))
    full = _const_spec((m, D_MODEL))
    return pl.pallas_call(
        _gmlp_step_body,
        grid=(1,),
        in_specs=[full, row, _const_spec(w_in.shape), row, row, row, row, _const_spec(w_out.shape)],
        out_specs=[full, full],
        out_shape=[jax.ShapeDtypeStruct((m, D_MODEL), F32), jax.ShapeDtypeStruct((m, D_MODEL), F32)],
        compiler_params=_cparams(("arbitrary",), 48),
        name="gmlp_step",
    )(x, g.reshape(1, D_MODEL), w_in, ln_g.reshape(1, D_MODEL), ln_b.reshape(1, D_MODEL), scale, bias, w_out)


def _rows_last(a):
    nd = a.ndim
    perm = tuple(range(nd - 4)) + (nd - 3, nd - 2, nd - 1, nd - 4)
    a = a.transpose(perm)
    return a.reshape(a.shape[:nd - 3] + (N_KV_B * HD_B, a.shape[-1]))


def kernel(x_prompt, x_sample, cache_cmp_kv, cache_sel_kv, state_win_kv, state_hgrn, page_table, norm_mix, norm_ffn, norm_final, w_in_ab, w_out_ab, hgrn_lower_bounds, hgrn_norm, cmp_pe, cmp_w1, cmp_w2, w_in_c, ln_c_g, ln_c_b, w_s, b_s, w_out_c, w_ffn1, w_ffn2):
    bp, seq, d = x_prompt.shape
    bs = x_sample.shape[0]
    depth = norm_mix.shape[0]
    n_ab = w_in_ab.shape[0]
    npg = page_table.shape[1]
    past_len = npg * PAGE_SIZE
    hp = x_prompt.reshape(bp * seq, d)
    hs = x_sample.reshape(bs, d)

    prompt_pages = jnp.broadcast_to(jnp.arange(seq // PAGE_SIZE, dtype=I32), (bp, seq // PAGE_SIZE))
    n_cmp_p = seq // CMP_STRIDE - 1
    n_cmp_s = (past_len + 1) // CMP_STRIDE - 1
    n_sel_s = -(-(past_len + 1) // SEL_BLOCK)
    cache_cmp_t = _rows_last(cache_cmp_kv)
    cache_sel_t = _rows_last(cache_sel_kv)
    win_t = _rows_last(state_win_kv)
    kv_ofs = AB_SEGS[2][0]

    kvt_p = None
    cmp_s, sel_s, win_s, hg_p, hg_s, cv_s = [], [], [], [], [], []
    for layer in range(depth):
        j = layer // 2
        if layer % 2 == 0:
            w_full = w_in_ab[j]
            w_in = jnp.pad(w_full, ((0, 0), (0, AB_COLS_PAD - AB_COLS))).astype(BF16)
            w_zqg = jnp.concatenate([w_in[:, :kv_ofs], w_in[:, kv_ofs + 3 * KV_COLS:]], axis=1)
            w_kv_t = w_full[:, kv_ofs:kv_ofs + 3 * KV_COLS].T.astype(BF16)
            w_out = w_out_ab[j].astype(BF16)
            w1x, pex, w2xt = _compress_weights(cmp_pe[j], cmp_w1[j], cmp_w2[j])

            za, q, gt, kc_t, ks_t, kw_t = in_proj_ab_t(hp, norm_mix[layer], w_zqg, w_kv_t, j, n_ab, bp, kvt_p)
            kvt_p = (kc_t, ks_t, kw_t)
            o_a, s_fin = hgrn_prompt(za, hgrn_lower_bounds, hgrn_norm[j], j, bp)
            kct = compress_pages(prompt_pages, kc_t.reshape(n_ab, bp, 2, KV_COLS // 2, seq), j, True,
                                 w1x, pex, w2xt, npp=min(16, seq // PAGE_SIZE))
            o_b = nsa_prompt(q, kct, ks_t, kw_t, gt, j, bp, n_cmp_p)
            hp = out_proj(o_a, o_b, w_out, hp)
            hg_p.append(s_fin)

            za, q, kvc, kvs, kvw, gt = in_proj_ab(hs, norm_mix[layer], w_in)
            o_a, s_new = hgrn_step(za, state_hgrn, hgrn_lower_bounds, hgrn_norm[j], j)
            kct = compress_pages(page_table, cache_cmp_t, j, False, w1x, pex, w2xt, npp=min(16, npg))
            q8 = q.reshape(bs, N_KV_B * REP_B, HD_B)
            o_cmp, idx = nsa_step_select(q8, kct, n_cmp_s, n_sel_s, past_len)
            o_b = nsa_step_attend(page_table, idx, cache_sel_t, win_t, j, q8, kvs.reshape(bs, 1, KV_COLS),
                                  kvw.reshape(bs, 1, KV_COLS), o_cmp,
                                  gt[:, :3 * N_KV_B * REP_B].reshape(bs, N_KV_B * REP_B, 3), past_len, past_len)
            hs = out_proj(o_a, o_b.reshape(bs, D_B), w_out, hs)
            kv6s = lambda a: a.reshape(bs, 1, 2, N_KV_B, HD_B)
            cmp_s.append(kv6s(kvc)); sel_s.append(kv6s(kvs)); hg_s.append(s_new)
            win_s.append(jnp.concatenate([state_win_kv[j], kv6s(kvw)], axis=1)[:, 1:])
        else:
            w_in = w_in_c[j].astype(BF16)
            w_out = w_out_c[j].astype(BF16)
            hp = gmlp_prompt(hp, norm_mix[layer], w_in, ln_c_g[j], ln_c_b[j], w_s[j], b_s[j], w_out)
            hs, v_new = gmlp_step(hs, norm_mix[layer], w_in, ln_c_g[j], ln_c_b[j], w_s[j], b_s[j], w_out)
            cv_s.append(v_new.reshape(bs, 1, d))
        w1 = w_ffn1[layer].astype(BF16)
        w2 = w_ffn2[layer].astype(BF16)
        final = layer == depth - 1
        hp = ffn(hp, norm_ffn[layer], w1, w2, norm_final, final)
        hs = ffn(hs, norm_ffn[layer], w1, w2, norm_final, final)

    rows_first = lambda a: a.reshape(n_ab, bp, 2, N_KV_B, HD_B, seq).transpose(0, 1, 5, 2, 3, 4)
    cmp_p, sel_p, win_p = (rows_first(a) for a in kvt_p)
    return (hp.reshape(bp, seq, d), hs.reshape(bs, 1, d), cmp_p, jnp.stack(cmp_s),
            sel_p, jnp.stack(sel_s), win_p[:, :, seq - min(WINDOW, seq):], jnp.stack(win_s),
            jnp.stack(hg_p), jnp.stack(hg_s), jnp.stack(cv_s))
```

```python
import functools

import numpy as np
import jax
import jax.numpy as jnp
from jax import lax
from jax.experimental import pallas as pl
from jax.experimental.pallas import tpu as pltpu

F32 = jnp.float32
BF16 = jnp.bfloat16
I32 = jnp.int32

D_MODEL = 1024
H_A = 4
DK_A = 128
D_A = 512
D_B = 512
HD_B = 64
N_KV_B = 2
REP_B = 4
CMP_BLOCK = 32
CMP_STRIDE = 16
CMP_HIDDEN = 128
SEL_BLOCK = 64
N_SEL = 16
N_LOCAL = 2
WINDOW = 512
PAGE_SIZE = 128
CHUNK_C = 128
G_C = 8
D_FF = 4096
KV_COLS = 2 * N_KV_B * HD_B
ATT_SCALE = HD_B ** -0.5
EPS = 1e-6
BIG = 1e6
NEG = -1e30
NEG_MASK = -3e38

AB_SEGS = ((0, 2048), (2048, 2560), (2560, 2816), (2816, 3072), (3072, 3328), (3328, 3456))
AB_COLS = 3352
AB_COLS_PAD = 3456

HGRN_CHUNK = 128
HGRN_CHUNKS_PER_STEP = 2
LANES = 128


def _cparams(sem, vmem_mb=None):
    kw = dict(dimension_semantics=sem)
    if vmem_mb is not None:
        kw["vmem_limit_bytes"] = vmem_mb << 20
    return pltpu.CompilerParams(**kw)


def _const_spec(shape):
    return pl.BlockSpec(shape, lambda *a: (0,) * len(shape))


def _dot(a, b):
    return jnp.dot(a, b, preferred_element_type=F32)


def _dot_nt(a, b):
    return lax.dot_general(a, b, (((1,), (1,)), ((), ())), preferred_element_type=F32)


def _split3(x):
    hi = x.astype(BF16)
    r = x - hi.astype(F32)
    mid = r.astype(BF16)
    lo = (r - mid.astype(F32)).astype(BF16)
    return hi, mid, lo


def _dot_01_lhs(m01, x):
    hi, mid, lo = _split3(x)
    return _dot(m01, hi) + _dot(m01, mid) + _dot(m01, lo)


def _dot_01_rhs(x, m01):
    hi, mid, lo = _split3(x)
    return _dot(hi, m01) + _dot(mid, m01) + _dot(lo, m01)


def _rms(x, g):
    return x * lax.rsqrt(jnp.mean(x * x, axis=-1, keepdims=True) + EPS) * g


def _gelu(x):
    return 0.5 * x * (1.0 + jnp.tanh(0.7978845608028654 * (x + 0.044715 * (x * x * x))))


def _sigmoid(x):
    return 1.0 / (1.0 + jnp.exp(-x))


def _silu(x):
    return x * _sigmoid(x)


def _softmax_masked(s, mask):
    sm = jnp.where(mask, s, NEG)
    m = jnp.max(sm, axis=-1, keepdims=True)
    e = jnp.where(mask, jnp.exp(sm - m), 0.0)
    return e / jnp.maximum(jnp.sum(e, axis=-1, keepdims=True), 1e-30)


def _in_proj_body(x_ref, g_ref, w_ref, *outs):
    xn = _rms(x_ref[...], g_ref[...]).astype(BF16)
    for (lo, hi), o_ref in zip(AB_SEGS, outs):
        o_ref[...] = _dot(xn, w_ref[:, lo:hi]).astype(o_ref.dtype)


def in_proj_ab(x, g, w):
    m = x.shape[0]
    tm = min(m, 512)
    widths = [hi - lo for lo, hi in AB_SEGS]
    dtypes = [F32, BF16, F32, F32, F32, F32]
    return pl.pallas_call(
        _in_proj_body,
        grid=(m // tm,),
        in_specs=[pl.BlockSpec((tm, D_MODEL), lambda i: (i, 0)),
                  _const_spec((1, D_MODEL)),
                  _const_spec((D_MODEL, AB_COLS_PAD))],
        out_specs=[pl.BlockSpec((tm, w_), lambda i: (i, 0)) for w_ in widths],
        out_shape=[jax.ShapeDtypeStruct((m, w_), dt) for w_, dt in zip(widths, dtypes)],
        compiler_params=_cparams(("parallel",), 56),
        name="in_proj_ab",
    )(x, g.reshape(1, D_MODEL), w)


def _in_proj_t_body(n_alias, x_ref, g_ref, w_ref, wkv_ref, *refs):
    za_ref, q_ref, gt_ref, cmp_ref, sel_ref, win_ref = refs[n_alias:]
    xn = _rms(x_ref[...], g_ref[...]).astype(BF16)
    za_ref[...] = _dot(xn, w_ref[:, 0:2048])
    q_ref[...] = _dot(xn, w_ref[:, 2048:2560]).astype(q_ref.dtype)
    gt_ref[...] = _dot(xn, w_ref[:, 2560:2688])
    kvt = _dot_nt(wkv_ref[...], xn)
    for i, o_ref in enumerate((cmp_ref, sel_ref, win_ref)):
        o_ref[...] = kvt[i * KV_COLS:(i + 1) * KV_COLS]


def in_proj_ab_t(x, g, w, wkv_t, j, n_ab, batch, prev):
    m = x.shape[0]
    t = m // batch
    tm = 512
    nt = t // tm
    n_alias = 0 if prev is None else 3
    row = lambda w_: pl.BlockSpec((tm, w_), lambda b, i: (b * nt + i, 0))
    kv_spec = pl.BlockSpec((None, None, KV_COLS, tm), lambda b, i: (j, b, 0, i))
    kv_shape = jax.ShapeDtypeStruct((n_ab, batch, KV_COLS, t), F32)
    return pl.pallas_call(
        functools.partial(_in_proj_t_body, n_alias),
        grid=(batch, nt),
        in_specs=[row(D_MODEL), _const_spec((1, D_MODEL)), _const_spec(w.shape), _const_spec(wkv_t.shape)]
        + [pl.BlockSpec(memory_space=pl.ANY)] * n_alias,
        out_specs=[row(2048), row(D_B), row(LANES), kv_spec, kv_spec, kv_spec],
        out_shape=[jax.ShapeDtypeStruct((m, 2048), F32), jax.ShapeDtypeStruct((m, D_B), BF16),
                   jax.ShapeDtypeStruct((m, LANES), F32), kv_shape, kv_shape, kv_shape],
        input_output_aliases={4 + i: 3 + i for i in range(n_alias)},
        compiler_params=_cparams(("parallel", "parallel"), 56),
        name="in_proj_ab_t",
    )(x, g.reshape(1, D_MODEL), w, wkv_t, *(() if prev is None else prev))


def _lower_bound(raw, j):
    m = jnp.max(raw, axis=0, keepdims=True)
    e = jnp.exp(raw - m)
    sm = e / jnp.sum(e, axis=0, keepdims=True)
    cs = sm[0:1]
    for i in range(1, j + 1):
        cs = cs + sm[i:i + 1]
    return cs - sm[0:1]


def _hgrn_consts(c):
    t = np.arange(c)[:, None]
    u = np.arange(c)[None, :]
    mats = [u <= t, u > t]
    masks = [t == u]
    for lvl in range(1, int(np.log2(c)) + 1):
        m = 1 << lvl
        half = m >> 1
        mid = (t // m) * m + half
        upper = (t % m) >= half
        mats.append((upper & (u >= mid) & (u <= t)) | (~upper & (u > t) & (u <= mid - 1)))
        masks.append(((t // m) == (u // m)) & upper & ((u % m) < half))
    return (np.concatenate(mats, 0).astype(np.float32), np.stack(masks).astype(np.float32))


def _hgrn_prompt_body(j, za_ref, lbraw_ref, hn_ref, mst_ref, lm_ref, o_ref, sfin_ref, st_ref):
    c = HGRN_CHUNK
    ci = pl.program_id(1)

    @pl.when(ci == 0)
    def _():
        st_ref[...] = jnp.zeros(st_ref.shape, F32)

    lb = _lower_bound(lbraw_ref[...], j)
    nlev = lm_ref.shape[0]
    nch = za_ref.shape[0] // c
    pre = []
    for cc in range(nch):
        rows = slice(cc * c, (cc + 1) * c)
        fg = lb + (1.0 - lb) * _sigmoid(za_ref[rows, 512:1024])
        lf = jnp.log(fg)
        q = _silu(za_ref[rows, 0:512])
        k = 1.0 - fg
        v = za_ref[rows, 1024:1536]
        gg = _silu(za_ref[rows, 1536:2048])
        x_all = jnp.exp(_dot_01_lhs(mst_ref[...], lf))
        heads = []
        for h in range(H_A):
            sl = slice(h * DK_A, (h + 1) * DK_A)
            qh, kh, vh = q[:, sl], k[:, sl], v[:, sl]
            x = x_all[:, sl]
            att = lm_ref[0] * _dot_nt(qh.astype(BF16), kh.astype(BF16))
            for lvl in range(1, nlev):
                d = x[(1 + lvl) * c:(2 + lvl) * c]
                att = att + lm_ref[lvl] * _dot_nt((qh * d).astype(BF16), (kh * d).astype(BF16))
            heads.append(((qh * x[0:c]).astype(BF16), _dot(att.astype(BF16), vh.astype(BF16)),
                          (kh * x[c:2 * c]).astype(BF16), vh.T.astype(BF16), x[c - 1:c], gg[:, sl]))
        pre.append(heads)
    for cc in range(nch):
        for h in range(H_A):
            q_in, o_intra, k_out, v_t, decay, gate = pre[cc][h]
            st = st_ref[h]
            o = _dot_nt(q_in, st.astype(BF16)) + o_intra
            st_ref[h] = st * decay + _dot(v_t, k_out)
            on = _rms(o, hn_ref[...]) * gate
            o_ref[cc * c:(cc + 1) * c, h * DK_A:(h + 1) * DK_A] = on.astype(o_ref.dtype)

    @pl.when(ci == pl.num_programs(1) - 1)
    def _():
        for h in range(H_A):
            sfin_ref[h] = st_ref[h].T


def hgrn_prompt(za, lb_raw, hnorm, j, batch):
    m = za.shape[0]
    t = m // batch
    rows = HGRN_CHUNK * HGRN_CHUNKS_PER_STEP
    nchunk = t // rows
    mst, lm = _hgrn_consts(HGRN_CHUNK)
    c = rows
    return pl.pallas_call(
        functools.partial(_hgrn_prompt_body, j),
        grid=(batch, nchunk),
        in_specs=[pl.BlockSpec((c, 2048), lambda b, i: (b * nchunk + i, 0)),
                  _const_spec(lb_raw.shape),
                  _const_spec((1, DK_A)),
                  _const_spec(mst.shape),
                  _const_spec(lm.shape)],
        out_specs=[pl.BlockSpec((c, D_A), lambda b, i: (b * nchunk + i, 0)),
                   pl.BlockSpec((None, H_A, DK_A, DK_A), lambda b, i: (b, 0, 0, 0))],
        out_shape=[jax.ShapeDtypeStruct((m, D_A), BF16),
                   jax.ShapeDtypeStruct((batch, H_A, DK_A, DK_A), F32)],
        scratch_shapes=[pltpu.VMEM((H_A, DK_A, DK_A), F32)],
        compiler_params=_cparams(("parallel", "arbitrary"), 48),
        name="hgrn_prompt",
    )(za, lb_raw, hnorm.reshape(1, DK_A), jnp.asarray(mst, BF16), jnp.asarray(lm, F32))


def _hgrn_step_body(j, bt, za_ref, s_ref, lbraw_ref, hn_ref, o_ref, snew_ref):
    lb = _lower_bound(lbraw_ref[...], j)
    fg = lb + (1.0 - lb) * _sigmoid(za_ref[:, 512:1024])
    q = _silu(za_ref[:, 0:512])
    k = 1.0 - fg
    v = za_ref[:, 1024:1536]
    gg = _silu(za_ref[:, 1536:2048])
    zpad = jnp.zeros((LANES - bt, DK_A), F32)

    def cols(a):
        return jnp.concatenate([a, zpad], axis=0).T

    for h in range(H_A):
        sl = slice(h * DK_A, (h + 1) * DK_A)
        qc, fc, kc = cols(q[:, sl]), cols(fg[:, sl]), cols(k[:, sl])
        for b in range(bt):
            s_new = fc[:, b:b + 1] * s_ref[b, h] + kc[:, b:b + 1] * v[b:b + 1, sl]
            snew_ref[b, h] = s_new
            o = jnp.sum(qc[:, b:b + 1] * s_new, axis=0, keepdims=True)
            o_ref[b:b + 1, sl] = (_rms(o, hn_ref[...]) * gg[b:b + 1, sl]).astype(o_ref.dtype)


def hgrn_step(za, state, lb_raw, hnorm, j):
    nb = za.shape[0]
    bt = 8
    return pl.pallas_call(
        functools.partial(_hgrn_step_body, j, bt),
        grid=(nb // bt,),
        in_specs=[pl.BlockSpec((bt, 2048), lambda i: (i, 0)),
                  pl.BlockSpec((None, bt, H_A, DK_A, DK_A), lambda i: (j, i, 0, 0, 0)),
                  _const_spec(lb_raw.shape),
                  _const_spec((1, DK_A))],
        out_specs=[pl.BlockSpec((bt, D_A), lambda i: (i, 0)),
                   pl.BlockSpec((bt, H_A, DK_A, DK_A), lambda i: (i, 0, 0, 0))],
        out_shape=[jax.ShapeDtypeStruct((nb, D_A), BF16),
                   jax.ShapeDtypeStruct(state.shape[1:], F32)],
        compiler_params=_cparams(("parallel",)),
        name="hgrn_step",
    )(za, state, lb_raw, hnorm.reshape(1, DK_A))


def _compress_weights(pe, w1, w2):
    eye = jnp.eye(N_KV_B, dtype=F32)
    w1r = w1.reshape(2, CMP_BLOCK, HD_B, CMP_HIDDEN)
    w1x = jnp.einsum("krdh,gG->rkgdGh", w1r, eye).reshape(CMP_BLOCK, 2, N_KV_B * HD_B, N_KV_B * CMP_HIDDEN)
    pex = jnp.broadcast_to(pe.transpose(1, 0, 2)[:, :, None, :], (CMP_BLOCK, 2, N_KV_B, HD_B))
    pex = pex.reshape(CMP_BLOCK, 2, 1, N_KV_B * HD_B)
    gd, gh = N_KV_B * HD_B, N_KV_B * CMP_HIDDEN
    w1x = w1x.reshape(CMP_BLOCK // 2, 2, 2, gd, gh).transpose(0, 2, 1, 3, 4).reshape(CMP_BLOCK // 2, 2, 2 * gd, gh)
    pex = pex.reshape(CMP_BLOCK // 2, 2, 2, 1, gd).transpose(0, 2, 3, 1, 4).reshape(CMP_BLOCK // 2, 2, 1, 2 * gd)
    w2xt = jnp.einsum("khd,gG->kGdgh", w2, eye).reshape(2, N_KV_B * HD_B, N_KV_B * CMP_HIDDEN)
    return w1x.astype(BF16), pex.astype(F32), w2xt.astype(BF16)


def _compress_body(npp, nseg, *refs):
    pages = refs[1:1 + npp]
    w1_ref, pe_ref, w2_ref, o_ref, xk, xv, bbuf = refs[1 + npp:]
    s = pl.program_id(1)
    for i, pg in enumerate(pages):
        row0 = pl.multiple_of((s * npp + i) * PAGE_SIZE, PAGE_SIZE)
        xk[pl.ds(row0, PAGE_SIZE), :] = pg[0].T
        xv[pl.ds(row0, PAGE_SIZE), :] = pg[1].T

    @pl.when(s == pl.num_programs(1) - 1)
    def _():
        bbuf[pl.ds(nseg, 8), :] = jnp.zeros((8, N_KV_B * CMP_HIDDEN), F32)
        for kv, xbuf in enumerate((xk, xv)):
            acc_a = jnp.zeros((nseg, N_KV_B * CMP_HIDDEN), F32)
            acc_b = jnp.zeros((nseg, N_KV_B * CMP_HIDDEN), F32)
            npair = CMP_STRIDE // 2
            for p in range(npair):
                xr = jnp.concatenate([xbuf[pl.ds(2 * p, nseg, stride=CMP_STRIDE), :],
                                      xbuf[pl.ds(2 * p + 1, nseg, stride=CMP_STRIDE), :]], axis=1)
                acc_a = acc_a + _dot((xr + pe_ref[p, kv]).astype(BF16), w1_ref[p, kv])
                acc_b = acc_b + _dot((xr + pe_ref[npair + p, kv]).astype(BF16), w1_ref[npair + p, kv])
            bbuf[pl.ds(0, nseg), :] = acc_b
            hid = _gelu(acc_a + bbuf[pl.ds(1, nseg), :])
            o_ref[kv] = _dot_nt(w2_ref[kv], hid.astype(BF16))


def compress_pages(table, pages_t, j, lane_paged, w1x, pex, w2xt, npp):
    nb, npg = table.shape
    nseg = npg * PAGE_SIZE // CMP_STRIDE
    half = KV_COLS // 2
    if lane_paged:
        imap = lambda b, s, t, i: (j, b, 0, 0, t[jnp.minimum(b, nb - 1), s * npp + i])
    else:
        imap = lambda b, s, t, i: (j, t[jnp.minimum(b, nb - 1), s * npp + i], 0, 0, 0)
    page_specs = [pl.BlockSpec((None, None, 2, half, PAGE_SIZE), functools.partial(imap, i=i)) for i in range(npp)]
    grid_spec = pltpu.PrefetchScalarGridSpec(
        num_scalar_prefetch=1,
        grid=(nb, npg // npp),
        in_specs=page_specs + [_const_spec(w1x.shape), _const_spec(pex.shape), _const_spec(w2xt.shape)],
        out_specs=pl.BlockSpec((None, 2, half, nseg), lambda b, s, t: (b, 0, 0, 0)),
        scratch_shapes=[pltpu.VMEM((npg * PAGE_SIZE, half), F32),
                        pltpu.VMEM((npg * PAGE_SIZE, half), F32),
                        pltpu.VMEM((nseg + 8, N_KV_B * CMP_HIDDEN), F32)],
    )
    return pl.pallas_call(
        functools.partial(_compress_body, npp, nseg),
        grid_spec=grid_spec,
        out_shape=jax.ShapeDtypeStruct((nb, 2, half, nseg), F32),
        compiler_params=_cparams(("parallel", "arbitrary"), 56),
        name="compress_pages",
    )(table, *([pages_t] * npp), w1x, pex, w2xt)


def _overlap_matrix(n_cmp_pad, n_sel_pad, n_cmp, n_sel):
    cs = np.arange(n_cmp_pad)[:, None] * CMP_STRIDE
    ss = np.arange(n_sel_pad)[None, :] * SEL_BLOCK
    ov = (cs < ss + SEL_BLOCK) & (cs + CMP_BLOCK > ss)
    ov &= (np.arange(n_cmp_pad)[:, None] < n_cmp) & (np.arange(n_sel_pad)[None, :] < n_sel)
    return ov.astype(np.float32)


def _rank(imp, jidx, cur):
    valid = jidx <= cur
    dist = cur - jidx
    forced = (jidx == 0) | ((dist >= 0) & (dist < N_LOCAL))
    return jnp.where(valid, imp + BIG * forced.astype(F32), -BIG)


NSA_TQ = 512
NSA_TK = 512
NSA_TW = 256
NSA_RB = 64
NSA_WB = 256


def _nsa_prompt_body(n_cmp, n_sel, q_ref, kct_ref, vct_ref, kst_ref, vst_ref, kwt_ref, vwt_ref, gt_ref, ov_ref,
                     eb_ref, o_ref, kc2, vc2e, vc2o, k2e, k2o, v2e, v2o, kw2, vw2e, vw2o, s_sc, e_sc, m_sc, acc_sc,
                     sw_sc, ew_sc, cnt_sc):
    tq, tk, tw, rb_ = NSA_TQ, NSA_TK, NSA_TW, NSA_RB
    hd = HD_B
    g = pl.program_id(1)
    qi = pl.program_id(2)
    t0 = qi * tq
    t_len = kst_ref.shape[1]
    ncp = kct_ref.shape[1]

    @pl.when(qi == 0)
    def _():
        zc = jnp.zeros((hd, ncp), BF16)
        kct = kct_ref[...].astype(BF16)
        vct = vct_ref[...].astype(BF16)
        kc2[0:hd] = kct
        kc2[hd:2 * hd] = kct
        vc2e[0:hd] = vct
        vc2e[hd:2 * hd] = zc
        vc2o[0:hd] = zc
        vc2o[hd:2 * hd] = vct
        ones_row = (lax.broadcasted_iota(I32, (hd, tk), 0) == 0).astype(BF16)
        for kt in range(t_len // tk):
            cols = slice(kt * tk, (kt + 1) * tk)
            kt_ = kst_ref[:, cols].astype(BF16)
            vt_ = vst_ref[:, cols].astype(BF16)
            k2e[kt, 0:hd] = kt_
            k2e[kt, hd:2 * hd] = eb_ref[kt]
            k2o[kt, 0:hd] = eb_ref[kt]
            k2o[kt, hd:2 * hd] = kt_
            v2e[kt, 0:hd] = vt_
            v2e[kt, hd:2 * hd] = ones_row
            v2o[kt, 0:hd] = ones_row
            v2o[kt, hd:2 * hd] = vt_
        zw = (lax.broadcasted_iota(I32, (hd, tw), 0) == 0).astype(BF16)
        for wt in range(t_len // tw):
            cols = slice(wt * tw, (wt + 1) * tw)
            kt_ = kwt_ref[:, cols].astype(BF16)
            vt_ = vwt_ref[:, cols].astype(BF16)
            kw2[wt, 0:hd] = kt_
            kw2[wt, hd:2 * hd] = kt_
            vw2e[wt, 0:hd] = vt_
            vw2e[wt, hd:2 * hd] = zw
            vw2o[wt, 0:hd] = zw
            vw2o[wt, hd:2 * hd] = vt_

    lane = lax.broadcasted_iota(I32, (1, LANES), 1)
    even = lane < hd
    tpos = t0 + lax.broadcasted_iota(I32, (tq, 1), 0)
    gts = _sigmoid(gt_ref[...])
    gsel = jnp.where(g == 0, gts[:, 0:REP_B * 3], gts[:, REP_B * 3:2 * REP_B * 3])
    gate = lambda r, c: gsel[:, 3 * r + c:3 * r + c + 1]
    zq = jnp.zeros((tq, LANES), q_ref.dtype)
    q128 = [q_ref[:, p * LANES:(p + 1) * LANES] * ATT_SCALE for p in range(REP_B // 2)]
    qh = [jnp.where(even if r % 2 == 0 else ~even, q128[r // 2], zq) for r in range(REP_B)]

    n_idx = lax.broadcasted_iota(I32, (1, ncp), 1)
    cmask = ((n_idx * CMP_STRIDE + (CMP_BLOCK - 1)) <= tpos) & (n_idx < n_cmp)
    o_cmp = []
    psum = jnp.zeros((tq, ncp), F32)
    for r in range(REP_B):
        p = _softmax_masked(_dot(qh[r], kc2[...]), cmask)
        psum = psum + p
        o_cmp.append(_dot_nt(p.astype(BF16), (vc2e if r % 2 == 0 else vc2o)[...]))
    imp = _dot_01_rhs(psum, ov_ref[...])

    imp_t = imp.T[0:n_sel]
    jidx = lax.broadcasted_iota(I32, (n_sel, tq), 0)
    cur = jnp.right_shift(t0 + lax.broadcasted_iota(I32, (n_sel, tq), 1), 6)
    rank = _rank(imp_t, jidx, cur)
    cnt_sc[...] = jnp.zeros(cnt_sc.shape, I32)
    grp_n = 8
    for grp in range(n_sel // grp_n):
        @pl.when(grp * grp_n * SEL_BLOCK < t0 + tq)
        def _():
            c = cnt_sc[...]
            for i in range(grp * grp_n, (grp + 1) * grp_n):
                ri = rank[i:i + 1]
                c = c + ((ri > rank) | ((ri == rank) & (jidx > i))).astype(I32)
            cnt_sc[...] = c
    cnt = cnt_sc[...]
    drop_t = (cnt >= min(N_SEL, n_sel)).astype(F32)
    if n_sel < hd:
        drop_t = jnp.concatenate([drop_t, jnp.zeros((hd - n_sel, tq), F32)], axis=0)
    drop = jnp.concatenate([drop_t, drop_t], axis=0).T.astype(q_ref.dtype)
    q2 = [jnp.where(even if r % 2 == 0 else ~even, q128[r // 2], drop) for r in range(REP_B)]

    m_sc[...] = jnp.full(m_sc.shape, NEG_MASK, F32)
    acc_sc[...] = jnp.zeros(acc_sc.shape, F32)

    def kv_tile(kt, causal):
        k0 = kt * tk
        kpos = k0 + lax.broadcasted_iota(I32, (1, tk), 1)
        for r in range(REP_B):
            s_sc[r] = _dot(q2[r], (k2e if r % 2 == 0 else k2o)[kt])
        for r in range(REP_B):
            for rb in range(tq // rb_):
                rows = slice(rb * rb_, (rb + 1) * rb_)
                sb = s_sc[r, rows]
                if causal:
                    sb = jnp.where(kpos <= tpos[rows], sb, NEG_MASK)
                m_old = m_sc[r, rows]
                m_new = jnp.maximum(m_old, jnp.max(sb, axis=-1, keepdims=True))
                m_sc[r, rows] = m_new
                e_sc[r, rows] = jnp.exp(sb - m_new).astype(e_sc.dtype)
                acc_sc[r, rows] = jnp.exp(m_old - m_new) * acc_sc[r, rows]
        for r in range(REP_B):
            acc_sc[r] += _dot_nt(e_sc[r], (v2e if r % 2 == 0 else v2o)[kt])

    n_full = t0 // tk

    def full_tile(kt, carry):
        kv_tile(kt, False)
        return carry

    lax.fori_loop(0, n_full, full_tile, 0)
    kv_tile(n_full, True)
    def normalised(a, r):
        own = even if r % 2 == 0 else ~even
        l = a[:, hd:hd + 1] if r % 2 == 0 else a[:, 0:1]
        return jnp.where(own, a / jnp.maximum(l, 1e-30), 0.0)

    o_sel = [normalised(acc_sc[r], r) for r in range(REP_B)]

    wb = NSA_WB
    nwt = (WINDOW + wb) // tw
    o_win = [[] for _ in range(REP_B)]
    for hb in range(tq // wb):
        rows = slice(hb * wb, (hb + 1) * wb)
        w0 = jnp.maximum(t0 + hb * wb - WINDOW, 0)
        wt0 = w0 // tw
        wpos = w0 + lax.broadcasted_iota(I32, (1, nwt * tw), 1)
        for r in range(REP_B):
            for c in range(nwt):
                sw_sc[r, c] = _dot(qh[r][rows], kw2[wt0 + c])
        for r in range(REP_B):
            for rb in range(wb // rb_):
                rr = slice(rb * rb_, (rb + 1) * rb_)
                dist = tpos[hb * wb + rb * rb_:hb * wb + (rb + 1) * rb_] - wpos
                wmask = (dist >= 0) & (dist < WINDOW)
                sb = jnp.where(wmask, jnp.concatenate([sw_sc[r, c, rr] for c in range(nwt)], axis=1), NEG)
                e = jnp.exp(sb - jnp.max(sb, axis=-1, keepdims=True)).astype(ew_sc.dtype)
                for c in range(nwt):
                    ew_sc[r, c, rr] = e[:, c * tw:(c + 1) * tw]
        for r in range(REP_B):
            vw = vw2e if r % 2 == 0 else vw2o
            ow = _dot_nt(ew_sc[r, 0], vw[wt0])
            for c in range(1, nwt):
                ow = ow + _dot_nt(ew_sc[r, c], vw[wt0 + c])
            o_win[r].append(normalised(ow, r))
    o_win = [jnp.concatenate(parts, axis=0) for parts in o_win]

    for p in range(REP_B // 2):
        acc = jnp.zeros((tq, LANES), F32)
        for r in (2 * p, 2 * p + 1):
            acc = acc + gate(r, 0) * o_cmp[r] + gate(r, 1) * o_sel[r] + gate(r, 2) * o_win[r]
        o_ref[:, p * LANES:(p + 1) * LANES] = acc.astype(o_ref.dtype)


def nsa_prompt(q, kct, kvt_sel, kvt_win, gates, j, batch, n_cmp):
    m = q.shape[0]
    t = m // batch
    tq, tk, tw = NSA_TQ, NSA_TK, NSA_TW
    hd = HD_B
    n_sel = t // SEL_BLOCK
    ncp = kct.shape[3]
    nqt = t // tq
    ov = jnp.asarray(_overlap_matrix(ncp, LANES, n_cmp, n_sel), BF16)
    kb = np.arange(t) // SEL_BLOCK
    eb = np.where(np.arange(hd)[:, None] == kb[None, :], NEG_MASK, 0.0).astype(np.float32)
    eb = jnp.asarray(eb.reshape(hd, t // tk, tk).transpose(1, 0, 2), BF16)
    n_ab = kvt_sel.shape[0]
    kct4 = kct.reshape(batch, 2 * N_KV_B, hd, ncp)
    sel5 = kvt_sel.reshape(n_ab, batch, 2 * N_KV_B, hd, t)
    win5 = kvt_win.reshape(n_ab, batch, 2 * N_KV_B, hd, t)
    cspec = lambda kv: pl.BlockSpec((None, None, hd, ncp), lambda b, g, i: (b, kv * N_KV_B + g, 0, 0))
    rspec = lambda kv: pl.BlockSpec((None, None, None, hd, t), lambda b, g, i: (j, b, kv * N_KV_B + g, 0, 0))
    slab = lambda n, w: pltpu.VMEM((n, 2 * hd, w), BF16)
    return pl.pallas_call(
        functools.partial(_nsa_prompt_body, n_cmp, n_sel),
        grid=(batch, N_KV_B, nqt),
        in_specs=[pl.BlockSpec((tq, REP_B * hd), lambda b, g, i: (b * nqt + i, g)),
                  cspec(0), cspec(1), rspec(0), rspec(1), rspec(0), rspec(1),
                  pl.BlockSpec((tq, LANES), lambda b, g, i: (b * nqt + i, 0)),
                  _const_spec(ov.shape), _const_spec(eb.shape)],
        out_specs=pl.BlockSpec((tq, REP_B * hd), lambda b, g, i: (b * nqt + i, g)),
        out_shape=jax.ShapeDtypeStruct((m, N_KV_B * REP_B * hd), BF16),
        scratch_shapes=[pltpu.VMEM((2 * hd, ncp), BF16), pltpu.VMEM((2 * hd, ncp), BF16), pltpu.VMEM((2 * hd, ncp), BF16),
                        slab(t // tk, tk), slab(t // tk, tk), slab(t // tk, tk), slab(t // tk, tk),
                        slab(t // tw, tw), slab(t // tw, tw), slab(t // tw, tw),
                        pltpu.VMEM((REP_B, tq, tk), F32), pltpu.VMEM((REP_B, tq, tk), BF16),
                        pltpu.VMEM((REP_B, tq, 1), F32), pltpu.VMEM((REP_B, tq, LANES), F32),
                        pltpu.VMEM((REP_B, (WINDOW + NSA_WB) // tw, NSA_WB, tw), F32),
                        pltpu.VMEM((REP_B, (WINDOW + NSA_WB) // tw, NSA_WB, tw), BF16),
                        pltpu.VMEM((n_sel, tq), I32)],
        compiler_params=_cparams(("parallel", "parallel", "arbitrary"), 56),
        name="nsa_prompt",
    )(q, kct4, kct4, sel5, sel5, win5, win5, gates, ov, eb)


def _group_queries(q8, g):
    z = jnp.zeros_like(q8)
    return jnp.concatenate([q8, z] if g == 0 else [z, q8], axis=1)


def _nsa_step_select_body(q_pos, n_cmp, n_sel, q_ref, kct_ref, ov_ref, ocmp_ref, idx_ref):
    q8 = q_ref[...] * ATT_SCALE
    ncp = kct_ref.shape[2]
    nsp = ov_ref.shape[1]
    n_idx = lax.broadcasted_iota(I32, (1, ncp), 1)
    cmask = ((n_idx * CMP_STRIDE + (CMP_BLOCK - 1)) <= q_pos) & (n_idx < n_cmp)
    head = lax.broadcasted_iota(I32, (N_KV_B * REP_B, 1), 0)
    kct = kct_ref[0].astype(BF16)
    vct = kct_ref[1].astype(BF16)
    o_all = jnp.zeros((N_KV_B * REP_B, HD_B), F32)
    for g in range(N_KV_B):
        ingrp = (head // REP_B) == g
        p = _softmax_masked(_dot(_group_queries(q8, g), kct), cmask)
        o_g = _dot_nt(p.astype(BF16), vct)
        o_all = jnp.where(ingrp, o_g[:, g * HD_B:(g + 1) * HD_B], o_all)
        imp8 = _dot_01_rhs(jnp.where(ingrp, p, 0.0), ov_ref[...])
        imp = jnp.sum(imp8, axis=0, keepdims=True)
        lane_j = lax.broadcasted_iota(I32, (nsp, nsp), 1)
        sub_i = lax.broadcasted_iota(I32, (nsp, nsp), 0)
        cur = q_pos // SEL_BLOCK
        rank_row = jnp.broadcast_to(_rank(imp, lane_j[0:1], cur), (nsp, nsp))
        rank_col = rank_row.T
        before = (rank_row > rank_col) | ((rank_row == rank_col) & (lane_j < sub_i))
        order = jnp.sum(before.astype(F32), axis=1, keepdims=True)
        slot = lax.broadcasted_iota(I32, (nsp, LANES), 1).astype(F32)
        hit = (order == slot) & (sub_i[:, 0:1] < n_sel)
        blk = lax.broadcasted_iota(I32, (nsp, LANES), 0).astype(F32)
        idx = jnp.sum(jnp.where(hit, blk, 0.0), axis=0, keepdims=True)
        idx_ref[g:g + 1, :] = idx.astype(I32)
    ocmp_ref[...] = o_all


def nsa_step_select(q, kct, n_cmp, n_sel, q_pos):
    nb = q.shape[0]
    ncp = kct.shape[3]
    nsp = 256
    nh = N_KV_B * REP_B
    ov = jnp.asarray(_overlap_matrix(ncp, nsp, n_cmp, n_sel), BF16)
    return pl.pallas_call(
        functools.partial(_nsa_step_select_body, q_pos, n_cmp, n_sel),
        grid=(nb,),
        in_specs=[pl.BlockSpec((None, nh, HD_B), lambda b: (b, 0, 0)),
                  pl.BlockSpec((None, 2, KV_COLS // 2, ncp), lambda b: (b, 0, 0, 0)),
                  _const_spec(ov.shape)],
        out_specs=[pl.BlockSpec((None, nh, HD_B), lambda b: (b, 0, 0)),
                   pl.BlockSpec((None, N_KV_B, LANES), lambda b: (b, 0, 0))],
        out_shape=[jax.ShapeDtypeStruct((nb, nh, HD_B), F32),
                   jax.ShapeDtypeStruct((nb, N_KV_B, LANES), I32)],
        compiler_params=_cparams(("parallel",)),
        name="nsa_step_select",
    )(q, kct, ov)


def _attend_with_new(qp, kt, vt, kmask, k_new, v_new, new_ok):
    s = _dot(qp, kt)
    s_new = jnp.sum(qp.astype(F32) * k_new, axis=1, keepdims=True)
    sm = jnp.where(kmask, s, NEG)
    sn = jnp.where(new_ok, s_new, NEG)
    m = jnp.maximum(jnp.max(sm, axis=-1, keepdims=True), sn)
    e = jnp.where(kmask, jnp.exp(sm - m), 0.0)
    e_new = jnp.where(new_ok, jnp.exp(sn - m), 0.0)
    l = jnp.sum(e, axis=-1, keepdims=True) + e_new
    return (_dot_nt(e.astype(BF16), vt) + e_new * v_new) / jnp.maximum(l, 1e-30)


def _nsa_step_attend_body(q_pos, past_len, k_top, *refs):
    tbl_ref, idx_ref = refs[0], refs[1]
    pages = refs[2:2 + N_KV_B * k_top]
    (q_ref, ksn_ref, win_ref, kwn_ref, ocmp_ref, gt_ref, o_ref, kt_sc, vt_sc) = refs[2 + N_KV_B * k_top:]
    b = pl.program_id(0)
    nh = N_KV_B * REP_B
    half = KV_COLS // 2
    q8 = q_ref[...] * ATT_SCALE
    head = lax.broadcasted_iota(I32, (nh, 1), 0)
    nkeys = k_top * PAGE_SIZE
    wlen = win_ref.shape[2]
    wrow = lax.broadcasted_iota(I32, (1, wlen), 1)
    wdist = q_pos - (past_len - wlen + wrow)
    wmask = (wdist >= 0) & (wdist < WINDOW)
    kwt = win_ref[0].astype(BF16)
    vwt = win_ref[1].astype(BF16)
    ks_new, vs_new = ksn_ref[:, 0:half], ksn_ref[:, half:KV_COLS]
    kw_new, vw_new = kwn_ref[:, 0:half], kwn_ref[:, half:KV_COLS]
    o_sel = jnp.zeros((nh, HD_B), F32)
    o_win = jnp.zeros((nh, HD_B), F32)
    for g in range(N_KV_B):
        ingrp = (head // REP_B) == g
        qp = _group_queries(q8, g)
        has_new = jnp.zeros((), I32)
        krow = lax.broadcasted_iota(I32, (1, nkeys), 1)
        kmask = jnp.zeros((1, nkeys), jnp.bool_)
        for i in range(k_top):
            pg = pages[g * k_top + i]
            kt_sc[:, i * PAGE_SIZE:(i + 1) * PAGE_SIZE] = pg[0].astype(BF16)
            vt_sc[:, i * PAGE_SIZE:(i + 1) * PAGE_SIZE] = pg[1].astype(BF16)
            jb = idx_ref[b, g, i]
            kpos = (jb * SEL_BLOCK // PAGE_SIZE) * PAGE_SIZE + (krow - i * PAGE_SIZE)
            inblk = (krow >= i * PAGE_SIZE) & (krow < (i + 1) * PAGE_SIZE) & (kpos // SEL_BLOCK == jb)
            kmask = kmask | (inblk & (kpos <= q_pos) & (kpos < past_len))
            has_new = has_new | (jb * SEL_BLOCK + SEL_BLOCK > past_len).astype(I32)
        os_ = _attend_with_new(qp, kt_sc[...], vt_sc[...], kmask, ks_new, vs_new, has_new > 0)
        o_sel = jnp.where(ingrp, os_[:, g * HD_B:(g + 1) * HD_B], o_sel)
        ow_ = _attend_with_new(qp, kwt, vwt, wmask, kw_new, vw_new, True)
        o_win = jnp.where(ingrp, ow_[:, g * HD_B:(g + 1) * HD_B], o_win)
    gt = _sigmoid(gt_ref[...])
    o_ref[...] = (gt[:, 0:1] * ocmp_ref[...] + gt[:, 1:2] * o_sel + gt[:, 2:3] * o_win).astype(o_ref.dtype)


def nsa_step_attend(table, idx, cache_sel_t, win_t, j, q, kvs_new, kvw_new, o_cmp, gates, q_pos, past_len):
    nb = q.shape[0]
    k_top = N_SEL
    wlen = win_t.shape[4]
    half = KV_COLS // 2
    per_page = PAGE_SIZE // SEL_BLOCK
    last_cached = past_len // SEL_BLOCK - 1

    def page_map(b, t, ix, g, i):
        bb = jnp.minimum(b, nb - 1)
        jb = jnp.minimum(ix[bb, g, i], last_cached)
        return (j, t[bb, jb // per_page], 0, 0, 0)

    page_specs = [pl.BlockSpec((None, None, 2, half, PAGE_SIZE), functools.partial(page_map, g=g, i=i))
                  for g in range(N_KV_B) for i in range(k_top)]
    nh = N_KV_B * REP_B
    row_spec = lambda w: pl.BlockSpec((None, nh, w), lambda b, t, ix: (b, 0, 0))
    new_spec = pl.BlockSpec((None, 1, KV_COLS), lambda b, t, ix: (b, 0, 0))
    grid_spec = pltpu.PrefetchScalarGridSpec(
        num_scalar_prefetch=2,
        grid=(nb,),
        in_specs=page_specs + [row_spec(HD_B), new_spec,
                               pl.BlockSpec((None, None, 2, half, wlen), lambda b, t, ix: (j, b, 0, 0, 0)),
                               new_spec, row_spec(HD_B), row_spec(3)],
        out_specs=row_spec(HD_B),
        scratch_shapes=[pltpu.VMEM((half, k_top * PAGE_SIZE), BF16),
                        pltpu.VMEM((half, k_top * PAGE_SIZE), BF16)],
    )
    return pl.pallas_call(
        functools.partial(_nsa_step_attend_body, q_pos, past_len, k_top),
        grid_spec=grid_spec,
        out_shape=jax.ShapeDtypeStruct((nb, nh, HD_B), BF16),
        compiler_params=_cparams(("arbitrary",)),
        name="nsa_step_attend",
    )(table, idx, *([cache_sel_t] * (N_KV_B * k_top)), q, kvs_new, win_t, kvw_new, o_cmp, gates)


def _out_proj_body(oa_ref, ob_ref, w_ref, r_ref, o_ref):
    o_ref[...] = r_ref[...] + _dot(oa_ref[...], w_ref[0:D_A, :]) + _dot(ob_ref[...], w_ref[D_A:D_A + D_B, :])


def out_proj(o_a, o_b, w, resid):
    m = resid.shape[0]
    tm = min(m, 512)
    return pl.pallas_call(
        _out_proj_body,
        grid=(m // tm,),
        in_specs=[pl.BlockSpec((tm, D_A), lambda i: (i, 0)), pl.BlockSpec((tm, D_B), lambda i: (i, 0)),
                  _const_spec(w.shape), pl.BlockSpec((tm, D_MODEL), lambda i: (i, 0))],
        out_specs=pl.BlockSpec((tm, D_MODEL), lambda i: (i, 0)),
        out_shape=jax.ShapeDtypeStruct((m, D_MODEL), F32),
        compiler_params=_cparams(("parallel",)),
        name="out_proj",
    )(o_a, o_b, w, resid)


def _ffn_body(final, x_ref, g_ref, w1_ref, w2_ref, gf_ref, o_ref, xn_sc, acc_sc):
    jf = pl.program_id(1)

    @pl.when(jf == 0)
    def _():
        xn_sc[...] = _rms(x_ref[...], g_ref[...]).astype(BF16)
        acc_sc[...] = jnp.zeros(acc_sc.shape, F32)

    h = jnp.maximum(_dot(xn_sc[...], w1_ref[...]), 0.0)
    acc_sc[...] += _dot((h * h).astype(BF16), w2_ref[...])

    @pl.when(jf == pl.num_programs(1) - 1)
    def _():
        y = x_ref[...] + acc_sc[...]
        o_ref[...] = _rms(y, gf_ref[...]) if final else y


def ffn(x, g, w1, w2, g_final, final):
    m = x.shape[0]
    tm = min(m, 1024)
    tf = 512
    return pl.pallas_call(
        functools.partial(_ffn_body, final),
        grid=(m // tm, D_FF // tf),
        in_specs=[pl.BlockSpec((tm, D_MODEL), lambda i, j: (i, 0)),
                  _const_spec((1, D_MODEL)),
                  pl.BlockSpec((D_MODEL, tf), lambda i, j: (0, j)),
                  pl.BlockSpec((tf, D_MODEL), lambda i, j: (j, 0)),
                  _const_spec((1, D_MODEL))],
        out_specs=pl.BlockSpec((tm, D_MODEL), lambda i, j: (i, 0)),
        out_shape=jax.ShapeDtypeStruct((m, D_MODEL), F32),
        scratch_shapes=[pltpu.VMEM((tm, D_MODEL), BF16), pltpu.VMEM((tm, D_MODEL), F32)],
        compiler_params=_cparams(("parallel", "arbitrary"), 48),
        name="ffn",
    )(x, g.reshape(1, D_MODEL), w1, w2, g_final.reshape(1, D_MODEL))


def _layernorm(v, g, b):
    mu = jnp.mean(v, axis=-1, keepdims=True)
    var = jnp.mean(jnp.square(v - mu), axis=-1, keepdims=True)
    return (v - mu) * lax.rsqrt(var + EPS) * g + b


def _gmlp_prompt_body(tm, x_ref, g_ref, win_ref, lng_ref, lnb_ref, ws_ref, bs_ref, wout_ref, o_ref, mix_sc):
    x = x_ref[...]
    z = _gelu(_dot(_rms(x, g_ref[...]).astype(BF16), win_ref[...]))
    u = z[:, 0:D_MODEL]
    v = _layernorm(z[:, D_MODEL:2 * D_MODEL], lng_ref[...], lnb_ref[...]).astype(BF16)
    row = lax.broadcasted_iota(I32, (CHUNK_C, CHUNK_C), 0)
    col = lax.broadcasted_iota(I32, (CHUNK_C, CHUNK_C), 1)
    gcd = D_MODEL // G_C
    for gi in range(G_C):
        ws = jnp.where(row >= col, ws_ref[gi], 0.0).astype(BF16)
        for ch in range(tm // CHUNK_C):
            rows = slice(ch * CHUNK_C, (ch + 1) * CHUNK_C)
            cols = slice(gi * gcd, (gi + 1) * gcd)
            mix_sc[rows, cols] = _dot(ws, v[rows, cols]) + bs_ref[gi]
    o_ref[...] = x + _dot((u * mix_sc[...]).astype(BF16), wout_ref[...])


def gmlp_prompt(x, g, w_in, ln_g, ln_b, w_s, b_s, w_out):
    m = x.shape[0]
    tm = 256
    return pl.pallas_call(
        functools.partial(_gmlp_prompt_body, tm),
        grid=(m // tm,),
        in_specs=[pl.BlockSpec((tm, D_MODEL), lambda i: (i, 0)),
                  _const_spec((1, D_MODEL)), _const_spec(w_in.shape),
                  _const_spec((1, D_MODEL)), _const_spec((1, D_MODEL)),
                  _const_spec(w_s.shape), _const_spec((G_C, CHUNK_C, 1)), _const_spec(w_out.shape)],
        out_specs=pl.BlockSpec((tm, D_MODEL), lambda i: (i, 0)),
        out_shape=jax.ShapeDtypeStruct((m, D_MODEL), F32),
        scratch_shapes=[pltpu.VMEM((tm, D_MODEL), F32)],
        compiler_params=_cparams(("parallel",), 48),
        name="gmlp_prompt",
    )(x, g.reshape(1, D_MODEL), w_in, ln_g.reshape(1, D_MODEL), ln_b.reshape(1, D_MODEL),
      w_s, b_s.reshape(G_C, CHUNK_C, 1), w_out)


def _gmlp_step_body(x_ref, g_ref, win_ref, lng_ref, lnb_ref, sc_ref, bi_ref, wout_ref, o_ref, v_ref):
    x = x_ref[...]
    z = _gelu(_dot(_rms(x, g_ref[...]).astype(BF16), win_ref[...]))
    v = _layernorm(z[:, D_MODEL:2 * D_MODEL], lng_ref[...], lnb_ref[...])
    v_ref[...] = v
    mix = v * sc_ref[...] + bi_ref[...]
    o_ref[...] = x + _dot((z[:, 0:D_MODEL] * mix).astype(BF16), wout_ref[...])


def gmlp_step(x, g, w_in, ln_g, ln_b, w_s, b_s, w_out):
    m = x.shape[0]
    gcd = D_MODEL // G_C
    scale = jnp.repeat(w_s[:, 0, 0], gcd).reshape(1, D_MODEL)
    bias = jnp.repeat(b_s[:, 0], gcd).reshape(1, D_MODEL)
    row = _const_spec((1, D_MODEL))
    full = _const_spec((m, D_MODEL))
    return pl.pallas_call(
        _gmlp_step_body,
        grid=(1,),
        in_specs=[full, row, _const_spec(w_in.shape), row, row, row, row, _const_spec(w_out.shape)],
        out_specs=[full, full],
        out_shape=[jax.ShapeDtypeStruct((m, D_MODEL), F32), jax.ShapeDtypeStruct((m, D_MODEL), F32)],
        compiler_params=_cparams(("arbitrary",), 48),
        name="gmlp_step",
    )(x, g.reshape(1, D_MODEL), w_in, ln_g.reshape(1, D_MODEL), ln_b.reshape(1, D_MODEL), scale, bias, w_out)


def _rows_last(a):
    nd = a.ndim
    perm = tuple(range(nd - 4)) + (nd - 3, nd - 2, nd - 1, nd - 4)
    a = a.transpose(perm)
    return a.reshape(a.shape[:nd - 3] + (N_KV_B * HD_B, a.shape[-1]))


def kernel(x_prompt, x_sample, cache_cmp_kv, cache_sel_kv, state_win_kv, state_hgrn, page_table, norm_mix, norm_ffn, norm_final, w_in_ab, w_out_ab, hgrn_lower_bounds, hgrn_norm, cmp_pe, cmp_w1, cmp_w2, w_in_c, ln_c_g, ln_c_b, w_s, b_s, w_out_c, w_ffn1, w_ffn2):
    bp, seq, d = x_prompt.shape
    bs = x_sample.shape[0]
    depth = norm_mix.shape[0]
    n_ab = w_in_ab.shape[0]
    npg = page_table.shape[1]
    past_len = npg * PAGE_SIZE
    hp = x_prompt.reshape(bp * seq, d)
    hs = x_sample.reshape(bs, d)

    prompt_pages = jnp.broadcast_to(jnp.arange(seq // PAGE_SIZE, dtype=I32), (bp, seq // PAGE_SIZE))
    n_cmp_p = seq // CMP_STRIDE - 1
    n_cmp_s = (past_len + 1) // CMP_STRIDE - 1
    n_sel_s = -(-(past_len + 1) // SEL_BLOCK)
    cache_cmp_t = _rows_last(cache_cmp_kv)
    cache_sel_t = _rows_last(cache_sel_kv)
    win_t = _rows_last(state_win_kv)
    kv_ofs = AB_SEGS[2][0]

    kvt_p = None
    cmp_s, sel_s, win_s, hg_p, hg_s, cv_s = [], [], [], [], [], []
    for layer in range(depth):
        j = layer // 2
        if layer % 2 == 0:
            w_full = w_in_ab[j]
            w_in = jnp.pad(w_full, ((0, 0), (0, AB_COLS_PAD - AB_COLS))).astype(BF16)
            w_zqg = jnp.concatenate([w_in[:, :kv_ofs], w_in[:, kv_ofs + 3 * KV_COLS:]], axis=1)
            w_kv_t = w_full[:, kv_ofs:kv_ofs + 3 * KV_COLS].T.astype(BF16)
            w_out = w_out_ab[j].astype(BF16)
            w1x, pex, w2xt = _compress_weights(cmp_pe[j], cmp_w1[j], cmp_w2[j])

            za, q, gt, kc_t, ks_t, kw_t = in_proj_ab_t(hp, norm_mix[layer], w_zqg, w_kv_t, j, n_ab, bp, kvt_p)
            kvt_p = (kc_t, ks_t, kw_t)
            o_a, s_fin = hgrn_prompt(za, hgrn_lower_bounds, hgrn_norm[j], j, bp)
            kct = compress_pages(prompt_pages, kc_t.reshape(n_ab, bp, 2, KV_COLS // 2, seq), j, True,
                                 w1x, pex, w2xt, npp=min(16, seq // PAGE_SIZE))
            o_b = nsa_prompt(q, kct, ks_t, kw_t, gt, j, bp, n_cmp_p)
            hp = out_proj(o_a, o_b, w_out, hp)
            hg_p.append(s_fin)

            za, q, kvc, kvs, kvw, gt = in_proj_ab(hs, norm_mix[layer], w_in)
            o_a, s_new = hgrn_step(za, state_hgrn, hgrn_lower_bounds, hgrn_norm[j], j)
            kct = compress_pages(page_table, cache_cmp_t, j, False, w1x, pex, w2xt, npp=min(16, npg))
            q8 = q.reshape(bs, N_KV_B * REP_B, HD_B)
            o_cmp, idx = nsa_step_select(q8, kct, n_cmp_s, n_sel_s, past_len)
            o_b = nsa_step_attend(page_table, idx, cache_sel_t, win_t, j, q8, kvs.reshape(bs, 1, KV_COLS),
                                  kvw.reshape(bs, 1, KV_COLS), o_cmp,
                                  gt[:, :3 * N_KV_B * REP_B].reshape(bs, N_KV_B * REP_B, 3), past_len, past_len)
            hs = out_proj(o_a, o_b.reshape(bs, D_B), w_out, hs)
            kv6s = lambda a: a.reshape(bs, 1, 2, N_KV_B, HD_B)
            cmp_s.append(kv6s(kvc)); sel_s.append(kv6s(kvs)); hg_s.append(s_new)
            win_s.append(jnp.concatenate([state_win_kv[j], kv6s(kvw)], axis=1)[:, 1:])
        else:
            w_in = w_in_c[j].astype(BF16)
            w_out = w_out_c[j].astype(BF16)
            hp = gmlp_prompt(hp, norm_mix[layer], w_in, ln_c_g[j], ln_c_b[j], w_s[j], b_s[j], w_out)
            hs, v_new = gmlp_step(hs, norm_mix[layer], w_in, ln_c_g[j], ln_c_b[j], w_s[j], b_s[j], w_out)
            cv_s.append(v_new.reshape(bs, 1, d))
        w1 = w_ffn1[layer].astype(BF16)
        w2 = w_ffn2[layer].astype(BF16)
        final = layer == depth - 1
        hp = ffn(hp, norm_ffn[layer], w1, w2, norm_final, final)
        hs = ffn(hs, norm_ffn[layer], w1, w2, norm_final, final)

    rows_first = lambda a: a.reshape(n_ab, bp, 2, N_KV_B, HD_B, seq).transpose(0, 1, 5, 2, 3, 4)
    cmp_p, sel_p, win_p = (rows_first(a) for a in kvt_p)
    return (hp.reshape(bp, seq, d), hs.reshape(bs, 1, d), cmp_p, jnp.stack(cmp_s),
            sel_p, jnp.stack(sel_s), win_p[:, :, seq - min(WINDOW, seq):], jnp.stack(win_s),
            jnp.stack(hg_p), jnp.stack(hg_s), jnp.stack(cv_s))
```

```python
import functools

import numpy as np
import jax
import jax.numpy as jnp
from jax import lax
from jax.experimental import pallas as pl
from jax.experimental.pallas import tpu as pltpu

F32 = jnp.float32
BF16 = jnp.bfloat16
I32 = jnp.int32

D_MODEL = 1024
H_A = 4
DK_A = 128
D_A = 512
D_B = 512
HD_B = 64
N_KV_B = 2
REP_B = 4
CMP_BLOCK = 32
CMP_STRIDE = 16
CMP_HIDDEN = 128
SEL_BLOCK = 64
N_SEL = 16
N_LOCAL = 2
WINDOW = 512
PAGE_SIZE = 128
CHUNK_C = 128
G_C = 8
D_FF = 4096
KV_COLS = 2 * N_KV_B * HD_B
ATT_SCALE = HD_B ** -0.5
EPS = 1e-6
BIG = 1e6
NEG = -1e30
NEG_MASK = -3e38

AB_SEGS = ((0, 2048), (2048, 2560), (2560, 2816), (2816, 3072), (3072, 3328), (3328, 3456))
AB_COLS = 3352
AB_COLS_PAD = 3456

HGRN_CHUNK = 128
HGRN_CHUNKS_PER_STEP = 2
LANES = 128


def _cparams(sem, vmem_mb=None):
    kw = dict(dimension_semantics=sem)
    if vmem_mb is not None:
        kw["vmem_limit_bytes"] = vmem_mb << 20
    return pltpu.CompilerParams(**kw)


def _const_spec(shape):
    return pl.BlockSpec(shape, lambda *a: (0,) * len(shape))


def _dot(a, b):
    return jnp.dot(a, b, preferred_element_type=F32)


def _dot_nt(a, b):
    return lax.dot_general(a, b, (((1,), (1,)), ((), ())), preferred_element_type=F32)


def _split3(x):
    hi = x.astype(BF16)
    r = x - hi.astype(F32)
    mid = r.astype(BF16)
    lo = (r - mid.astype(F32)).astype(BF16)
    return hi, mid, lo


def _dot_01_lhs(m01, x):
    hi, mid, lo = _split3(x)
    return _dot(m01, hi) + _dot(m01, mid) + _dot(m01, lo)


def _dot_01_rhs(x, m01):
    hi, mid, lo = _split3(x)
    return _dot(hi, m01) + _dot(mid, m01) + _dot(lo, m01)


def _rms(x, g):
    return x * lax.rsqrt(jnp.mean(x * x, axis=-1, keepdims=True) + EPS) * g


def _gelu(x):
    return 0.5 * x * (1.0 + jnp.tanh(0.7978845608028654 * (x + 0.044715 * (x * x * x))))


def _sigmoid(x):
    return 1.0 / (1.0 + jnp.exp(-x))


def _silu(x):
    return x * _sigmoid(x)


def _softmax_masked(s, mask):
    sm = jnp.where(mask, s, NEG)
    m = jnp.max(sm, axis=-1, keepdims=True)
    e = jnp.where(mask, jnp.exp(sm - m), 0.0)
    return e / jnp.maximum(jnp.sum(e, axis=-1, keepdims=True), 1e-30)


def _in_proj_body(x_ref, g_ref, w_ref, *outs):
    xn = _rms(x_ref[...], g_ref[...]).astype(BF16)
    for (lo, hi), o_ref in zip(AB_SEGS, outs):
        o_ref[...] = _dot(xn, w_ref[:, lo:hi]).astype(o_ref.dtype)


def in_proj_ab(x, g, w):
    m = x.shape[0]
    tm = min(m, 512)
    widths = [hi - lo for lo, hi in AB_SEGS]
    dtypes = [F32, BF16, F32, F32, F32, F32]
    return pl.pallas_call(
        _in_proj_body,
        grid=(m // tm,),
        in_specs=[pl.BlockSpec((tm, D_MODEL), lambda i: (i, 0)),
                  _const_spec((1, D_MODEL)),
                  _const_spec((D_MODEL, AB_COLS_PAD))],
        out_specs=[pl.BlockSpec((tm, w_), lambda i: (i, 0)) for w_ in widths],
        out_shape=[jax.ShapeDtypeStruct((m, w_), dt) for w_, dt in zip(widths, dtypes)],
        compiler_params=_cparams(("parallel",), 56),
        name="in_proj_ab",
    )(x, g.reshape(1, D_MODEL), w)


def _in_proj_t_body(n_alias, x_ref, g_ref, w_ref, wkv_ref, *refs):
    za_ref, q_ref, gt_ref, cmp_ref, sel_ref, win_ref = refs[n_alias:]
    xn = _rms(x_ref[...], g_ref[...]).astype(BF16)
    za_ref[...] = _dot(xn, w_ref[:, 0:2048])
    q_ref[...] = _dot(xn, w_ref[:, 2048:2560]).astype(q_ref.dtype)
    gt_ref[...] = _dot(xn, w_ref[:, 2560:2688])
    kvt = _dot_nt(wkv_ref[...], xn)
    for i, o_ref in enumerate((cmp_ref, sel_ref, win_ref)):
        o_ref[...] = kvt[i * KV_COLS:(i + 1) * KV_COLS]


def in_proj_ab_t(x, g, w, wkv_t, j, n_ab, batch, prev):
    m = x.shape[0]
    t = m // batch
    tm = 512
    nt = t // tm
    n_alias = 0 if prev is None else 3
    row = lambda w_: pl.BlockSpec((tm, w_), lambda b, i: (b * nt + i, 0))
    kv_spec = pl.BlockSpec((None, None, KV_COLS, tm), lambda b, i: (j, b, 0, i))
    kv_shape = jax.ShapeDtypeStruct((n_ab, batch, KV_COLS, t), F32)
    return pl.pallas_call(
        functools.partial(_in_proj_t_body, n_alias),
        grid=(batch, nt),
        in_specs=[row(D_MODEL), _const_spec((1, D_MODEL)), _const_spec(w.shape), _const_spec(wkv_t.shape)]
        + [pl.BlockSpec(memory_space=pl.ANY)] * n_alias,
        out_specs=[row(2048), row(D_B), row(LANES), kv_spec, kv_spec, kv_spec],
        out_shape=[jax.ShapeDtypeStruct((m, 2048), F32), jax.ShapeDtypeStruct((m, D_B), BF16),
                   jax.ShapeDtypeStruct((m, LANES), F32), kv_shape, kv_shape, kv_shape],
        input_output_aliases={4 + i: 3 + i for i in range(n_alias)},
        compiler_params=_cparams(("parallel", "parallel"), 56),
        name="in_proj_ab_t",
    )(x, g.reshape(1, D_MODEL), w, wkv_t, *(() if prev is None else prev))


def _lower_bound(raw, j):
    m = jnp.max(raw, axis=0, keepdims=True)
    e = jnp.exp(raw - m)
    sm = e / jnp.sum(e, axis=0, keepdims=True)
    cs = sm[0:1]
    for i in range(1, j + 1):
        cs = cs + sm[i:i + 1]
    return cs - sm[0:1]


def _hgrn_consts(c):
    t = np.arange(c)[:, None]
    u = np.arange(c)[None, :]
    mats = [u <= t, u > t]
    masks = [t == u]
    for lvl in range(1, int(np.log2(c)) + 1):
        m = 1 << lvl
        half = m >> 1
        mid = (t // m) * m + half
        upper = (t % m) >= half
        mats.append((upper & (u >= mid) & (u <= t)) | (~upper & (u > t) & (u <= mid - 1)))
        masks.append(((t // m) == (u // m)) & upper & ((u % m) < half))
    return (np.concatenate(mats, 0).astype(np.float32), np.stack(masks).astype(np.float32))


def _hgrn_prompt_body(j, za_ref, lbraw_ref, hn_ref, mst_ref, lm_ref, o_ref, sfin_ref, st_ref):
    c = HGRN_CHUNK
    ci = pl.program_id(1)

    @pl.when(ci == 0)
    def _():
        st_ref[...] = jnp.zeros(st_ref.shape, F32)

    lb = _lower_bound(lbraw_ref[...], j)
    nlev = lm_ref.shape[0]
    nch = za_ref.shape[0] // c
    pre = []
    for cc in range(nch):
        rows = slice(cc * c, (cc + 1) * c)
        fg = lb + (1.0 - lb) * _sigmoid(za_ref[rows, 512:1024])
        lf = jnp.log(fg)
        q = _silu(za_ref[rows, 0:512])
        k = 1.0 - fg
        v = za_ref[rows, 1024:1536]
        gg = _silu(za_ref[rows, 1536:2048])
        x_all = jnp.exp(_dot_01_lhs(mst_ref[...], lf))
        heads = []
        for h in range(H_A):
            sl = slice(h * DK_A, (h + 1) * DK_A)
            qh, kh, vh = q[:, sl], k[:, sl], v[:, sl]
            x = x_all[:, sl]
            att = lm_ref[0] * _dot_nt(qh.astype(BF16), kh.astype(BF16))
            for lvl in range(1, nlev):
                d = x[(1 + lvl) * c:(2 + lvl) * c]
                att = att + lm_ref[lvl] * _dot_nt((qh * d).astype(BF16), (kh * d).astype(BF16))
            heads.append(((qh * x[0:c]).astype(BF16), _dot(att.astype(BF16), vh.astype(BF16)),
                          (kh * x[c:2 * c]).astype(BF16), vh.T.astype(BF16), x[c - 1:c], gg[:, sl]))
        pre.append(heads)
    for cc in range(nch):
        for h in range(H_A):
            q_in, o_intra, k_out, v_t, decay, gate = pre[cc][h]
            st = st_ref[h]
            o = _dot_nt(q_in, st.astype(BF16)) + o_intra
            st_ref[h] = st * decay + _dot(v_t, k_out)
            on = _rms(o, hn_ref[...]) * gate
            o_ref[cc * c:(cc + 1) * c, h * DK_A:(h + 1) * DK_A] = on.astype(o_ref.dtype)

    @pl.when(ci == pl.num_programs(1) - 1)
    def _():
        for h in range(H_A):
            sfin_ref[h] = st_ref[h].T


def hgrn_prompt(za, lb_raw, hnorm, j, batch):
    m = za.shape[0]
    t = m // batch
    rows = HGRN_CHUNK * HGRN_CHUNKS_PER_STEP
    nchunk = t // rows
    mst, lm = _hgrn_consts(HGRN_CHUNK)
    c = rows
    return pl.pallas_call(
        functools.partial(_hgrn_prompt_body, j),
        grid=(batch, nchunk),
        in_specs=[pl.BlockSpec((c, 2048), lambda b, i: (b * nchunk + i, 0)),
                  _const_spec(lb_raw.shape),
                  _const_spec((1, DK_A)),
                  _const_spec(mst.shape),
                  _const_spec(lm.shape)],
        out_specs=[pl.BlockSpec((c, D_A), lambda b, i: (b * nchunk + i, 0)),
                   pl.BlockSpec((None, H_A, DK_A, DK_A), lambda b, i: (b, 0, 0, 0))],
        out_shape=[jax.ShapeDtypeStruct((m, D_A), BF16),
                   jax.ShapeDtypeStruct((batch, H_A, DK_A, DK_A), F32)],
        scratch_shapes=[pltpu.VMEM((H_A, DK_A, DK_A), F32)],
        compiler_params=_cparams(("parallel", "arbitrary"), 48),
        name="hgrn_prompt",
    )(za, lb_raw, hnorm.reshape(1, DK_A), jnp.asarray(mst, BF16), jnp.asarray(lm, F32))


def _hgrn_step_body(j, bt, za_ref, s_ref, lbraw_ref, hn_ref, o_ref, snew_ref):
    lb = _lower_bound(lbraw_ref[...], j)
    fg = lb + (1.0 - lb) * _sigmoid(za_ref[:, 512:1024])
    q = _silu(za_ref[:, 0:512])
    k = 1.0 - fg
    v = za_ref[:, 1024:1536]
    gg = _silu(za_ref[:, 1536:2048])
    zpad = jnp.zeros((LANES - bt, DK_A), F32)

    def cols(a):
        return jnp.concatenate([a, zpad], axis=0).T

    for h in range(H_A):
        sl = slice(h * DK_A, (h + 1) * DK_A)
        qc, fc, kc = cols(q[:, sl]), cols(fg[:, sl]), cols(k[:, sl])
        for b in range(bt):
            s_new = fc[:, b:b + 1] * s_ref[b, h] + kc[:, b:b + 1] * v[b:b + 1, sl]
            snew_ref[b, h] = s_new
            o = jnp.sum(qc[:, b:b + 1] * s_new, axis=0, keepdims=True)
            o_ref[b:b + 1, sl] = (_rms(o, hn_ref[...]) * gg[b:b + 1, sl]).astype(o_ref.dtype)


def hgrn_step(za, state, lb_raw, hnorm, j):
    nb = za.shape[0]
    bt = 8
    return pl.pallas_call(
        functools.partial(_hgrn_step_body, j, bt),
        grid=(nb // bt,),
        in_specs=[pl.BlockSpec((bt, 2048), lambda i: (i, 0)),
                  pl.BlockSpec((None, bt, H_A, DK_A, DK_A), lambda i: (j, i, 0, 0, 0)),
                  _const_spec(lb_raw.shape),
                  _const_spec((1, DK_A))],
        out_specs=[pl.BlockSpec((bt, D_A), lambda i: (i, 0)),
                   pl.BlockSpec((bt, H_A, DK_A, DK_A), lambda i: (i, 0, 0, 0))],
        out_shape=[jax.ShapeDtypeStruct((nb, D_A), BF16),
                   jax.ShapeDtypeStruct(state.shape[1:], F32)],
        compiler_params=_cparams(("parallel",)),
        name="hgrn_step",
    )(za, state, lb_raw, hnorm.reshape(1, DK_A))


def _compress_weights(pe, w1, w2):
    eye = jnp.eye(N_KV_B, dtype=F32)
    w1r = w1.reshape(2, CMP_BLOCK, HD_B, CMP_HIDDEN)
    w1x = jnp.einsum("krdh,gG->rkgdGh", w1r, eye).reshape(CMP_BLOCK, 2, N_KV_B * HD_B, N_KV_B * CMP_HIDDEN)
    pex = jnp.broadcast_to(pe.transpose(1, 0, 2)[:, :, None, :], (CMP_BLOCK, 2, N_KV_B, HD_B))
    pex = pex.reshape(CMP_BLOCK, 2, 1, N_KV_B * HD_B)
    gd, gh = N_KV_B * HD_B, N_KV_B * CMP_HIDDEN
    w1x = w1x.reshape(CMP_BLOCK // 2, 2, 2, gd, gh).transpose(0, 2, 1, 3, 4).reshape(CMP_BLOCK // 2, 2, 2 * gd, gh)
    pex = pex.reshape(CMP_BLOCK // 2, 2, 2, 1, gd).transpose(0, 2, 3, 1, 4).reshape(CMP_BLOCK // 2, 2, 1, 2 * gd)
    w2xt = jnp.einsum("khd,gG->kGdgh", w2, eye).reshape(2, N_KV_B * HD_B, N_KV_B * CMP_HIDDEN)
    return w1x.astype(BF16), pex.astype(F32), w2xt.astype(BF16)


def _compress_body(npp, nseg, *refs):
    pages = refs[1:1 + npp]
    w1_ref, pe_ref, w2_ref, o_ref, xk, xv, bbuf = refs[1 + npp:]
    s = pl.program_id(1)
    for i, pg in enumerate(pages):
        row0 = pl.multiple_of((s * npp + i) * PAGE_SIZE, PAGE_SIZE)
        xk[pl.ds(row0, PAGE_SIZE), :] = pg[0].T
        xv[pl.ds(row0, PAGE_SIZE), :] = pg[1].T

    @pl.when(s == pl.num_programs(1) - 1)
    def _():
        bbuf[pl.ds(nseg, 8), :] = jnp.zeros((8, N_KV_B * CMP_HIDDEN), F32)
        for kv, xbuf in enumerate((xk, xv)):
            acc_a = jnp.zeros((nseg, N_KV_B * CMP_HIDDEN), F32)
            acc_b = jnp.zeros((nseg, N_KV_B * CMP_HIDDEN), F32)
            npair = CMP_STRIDE // 2
            for p in range(npair):
                xr = jnp.concatenate([xbuf[pl.ds(2 * p, nseg, stride=CMP_STRIDE), :],
                                      xbuf[pl.ds(2 * p + 1, nseg, stride=CMP_STRIDE), :]], axis=1)
                acc_a = acc_a + _dot((xr + pe_ref[p, kv]).astype(BF16), w1_ref[p, kv])
                acc_b = acc_b + _dot((xr + pe_ref[npair + p, kv]).astype(BF16), w1_ref[npair + p, kv])
            bbuf[pl.ds(0, nseg), :] = acc_b
            hid = _gelu(acc_a + bbuf[pl.ds(1, nseg), :])
            o_ref[kv] = _dot_nt(w2_ref[kv], hid.astype(BF16))


def compress_pages(table, pages_t, j, lane_paged, w1x, pex, w2xt, npp):
    nb, npg = table.shape
    nseg = npg * PAGE_SIZE // CMP_STRIDE
    half = KV_COLS // 2
    if lane_paged:
        imap = lambda b, s, t, i: (j, b, 0, 0, t[jnp.minimum(b, nb - 1), s * npp + i])
    else:
        imap = lambda b, s, t, i: (j, t[jnp.minimum(b, nb - 1), s * npp + i], 0, 0, 0)
    page_specs = [pl.BlockSpec((None, None, 2, half, PAGE_SIZE), functools.partial(imap, i=i)) for i in range(npp)]
    grid_spec = pltpu.PrefetchScalarGridSpec(
        num_scalar_prefetch=1,
        grid=(nb, npg // npp),
        in_specs=page_specs + [_const_spec(w1x.shape), _const_spec(pex.shape), _const_spec(w2xt.shape)],
        out_specs=pl.BlockSpec((None, 2, half, nseg), lambda b, s, t: (b, 0, 0, 0)),
        scratch_shapes=[pltpu.VMEM((npg * PAGE_SIZE, half), F32),
                        pltpu.VMEM((npg * PAGE_SIZE, half), F32),
                        pltpu.VMEM((nseg + 8, N_KV_B * CMP_HIDDEN), F32)],
    )
    return pl.pallas_call(
        functools.partial(_compress_body, npp, nseg),
        grid_spec=grid_spec,
        out_shape=jax.ShapeDtypeStruct((nb, 2, half, nseg), F32),
        compiler_params=_cparams(("parallel", "arbitrary"), 56),
        name="compress_pages",
    )(table, *([pages_t] * npp), w1x, pex, w2xt)


def _overlap_matrix(n_cmp_pad, n_sel_pad, n_cmp, n_sel):
    cs = np.arange(n_cmp_pad)[:, None] * CMP_STRIDE
    ss = np.arange(n_sel_pad)[None, :] * SEL_BLOCK
    ov = (cs < ss + SEL_BLOCK) & (cs + CMP_BLOCK > ss)
    ov &= (np.arange(n_cmp_pad)[:, None] < n_cmp) & (np.arange(n_sel_pad)[None, :] < n_sel)
    return ov.astype(np.float32)


def _rank(imp, jidx, cur):
    valid = jidx <= cur
    dist = cur - jidx
    forced = (jidx == 0) | ((dist >= 0) & (dist < N_LOCAL))
    return jnp.where(valid, imp + BIG * forced.astype(F32), -BIG)


NSA_TQ = 512
NSA_TK = 512
NSA_TW = 256
NSA_RB = 64
NSA_WB = 256


def _nsa_prompt_body(n_cmp, n_sel, q_ref, kct_ref, vct_ref, kst_ref, vst_ref, kwt_ref, vwt_ref, gt_ref, ov_ref,
                     eb_ref, o_ref, kc2, vc2e, vc2o, k2e, k2o, v2e, v2o, kw2, vw2e, vw2o, s_sc, e_sc, m_sc, acc_sc,
                     sw_sc, ew_sc, cnt_sc):
    tq, tk, tw, rb_ = NSA_TQ, NSA_TK, NSA_TW, NSA_RB
    hd = HD_B
    g = pl.program_id(1)
    qi = pl.program_id(2)
    t0 = qi * tq
    t_len = kst_ref.shape[1]
    ncp = kct_ref.shape[1]

    @pl.when(qi == 0)
    def _():
        zc = jnp.zeros((hd, ncp), BF16)
        kct = kct_ref[...].astype(BF16)
        vct = vct_ref[...].astype(BF16)
        kc2[0:hd] = kct
        kc2[hd:2 * hd] = kct
        vc2e[0:hd] = vct
        vc2e[hd:2 * hd] = zc
        vc2o[0:hd] = zc
        vc2o[hd:2 * hd] = vct
        ones_row = (lax.broadcasted_iota(I32, (hd, tk), 0) == 0).astype(BF16)
        for kt in range(t_len // tk):
            cols = slice(kt * tk, (kt + 1) * tk)
            kt_ = kst_ref[:, cols].astype(BF16)
            vt_ = vst_ref[:, cols].astype(BF16)
            k2e[kt, 0:hd] = kt_
            k2e[kt, hd:2 * hd] = eb_ref[kt]
            k2o[kt, 0:hd] = eb_ref[kt]
            k2o[kt, hd:2 * hd] = kt_
            v2e[kt, 0:hd] = vt_
            v2e[kt, hd:2 * hd] = ones_row
            v2o[kt, 0:hd] = ones_row
            v2o[kt, hd:2 * hd] = vt_
        zw = (lax.broadcasted_iota(I32, (hd, tw), 0) == 0).astype(BF16)
        for wt in range(t_len // tw):
            cols = slice(wt * tw, (wt + 1) * tw)
            kt_ = kwt_ref[:, cols].astype(BF16)
            vt_ = vwt_ref[:, cols].astype(BF16)
            kw2[wt, 0:hd] = kt_
            kw2[wt, hd:2 * hd] = kt_
            vw2e[wt, 0:hd] = vt_
            vw2e[wt, hd:2 * hd] = zw
            vw2o[wt, 0:hd] = zw
            vw2o[wt, hd:2 * hd] = vt_

    lane = lax.broadcasted_iota(I32, (1, LANES), 1)
    even = lane < hd
    tpos = t0 + lax.broadcasted_iota(I32, (tq, 1), 0)
    gts = _sigmoid(gt_ref[...])
    gsel = jnp.where(g == 0, gts[:, 0:REP_B * 3], gts[:, REP_B * 3:2 * REP_B * 3])
    gate = lambda r, c: gsel[:, 3 * r + c:3 * r + c + 1]
    zq = jnp.zeros((tq, LANES), q_ref.dtype)
    q128 = [q_ref[:, p * LANES:(p + 1) * LANES] * ATT_SCALE for p in range(REP_B // 2)]
    qh = [jnp.where(even if r % 2 == 0 else ~even, q128[r // 2], zq) for r in range(REP_B)]

    n_idx = lax.broadcasted_iota(I32, (1, ncp), 1)
    cmask = ((n_idx * CMP_STRIDE + (CMP_BLOCK - 1)) <= tpos) & (n_idx < n_cmp)
    o_cmp = []
    psum = jnp.zeros((tq, ncp), F32)
    for r in range(REP_B):
        p = _softmax_masked(_dot(qh[r], kc2[...]), cmask)
        psum = psum + p
        o_cmp.append(_dot_nt(p.astype(BF16), (vc2e if r % 2 == 0 else vc2o)[...]))
    imp = _dot_01_rhs(psum, ov_ref[...])

    imp_t = imp.T[0:n_sel]
    jidx = lax.broadcasted_iota(I32, (n_sel, tq), 0)
    cur = jnp.right_shift(t0 + lax.broadcasted_iota(I32, (n_sel, tq), 1), 6)
    rank = _rank(imp_t, jidx, cur)
    cnt_sc[...] = jnp.zeros(cnt_sc.shape, I32)
    grp_n = 8
    for grp in range(n_sel // grp_n):
        @pl.when(grp * grp_n * SEL_BLOCK < t0 + tq)
        def _():
            c = cnt_sc[...]
            for i in range(grp * grp_n, (grp + 1) * grp_n):
                ri = rank[i:i + 1]
                c = c + ((ri > rank) | ((ri == rank) & (jidx > i))).astype(I32)
            cnt_sc[...] = c
    cnt = cnt_sc[...]
    drop_t = (cnt >= min(N_SEL, n_sel)).astype(F32)
    if n_sel < hd:
        drop_t = jnp.concatenate([drop_t, jnp.zeros((hd - n_sel, tq), F32)], axis=0)
    drop = jnp.concatenate([drop_t, drop_t], axis=0).T.astype(q_ref.dtype)
    q2 = [jnp.where(even if r % 2 == 0 else ~even, q128[r // 2], drop) for r in range(REP_B)]

    m_sc[...] = jnp.full(m_sc.shape, NEG_MASK, F32)
    acc_sc[...] = jnp.zeros(acc_sc.shape, F32)

    def kv_tile(kt, causal):
        k0 = kt * tk
        kpos = k0 + lax.broadcasted_iota(I32, (1, tk), 1)
        for r in range(REP_B):
            s_sc[r] = _dot(q2[r], (k2e if r % 2 == 0 else k2o)[kt])
        for r in range(REP_B):
            for rb in range(tq // rb_):
                rows = slice(rb * rb_, (rb + 1) * rb_)
                sb = s_sc[r, rows]
                if causal:
                    sb = jnp.where(kpos <= tpos[rows], sb, NEG_MASK)
                m_old = m_sc[r, rows]
                m_new = jnp.maximum(m_old, jnp.max(sb, axis=-1, keepdims=True))
                m_sc[r, rows] = m_new
                e_sc[r, rows] = jnp.exp(sb - m_new).astype(e_sc.dtype)
                acc_sc[r, rows] = jnp.exp(m_old - m_new) * acc_sc[r, rows]
        for r in range(REP_B):
            acc_sc[r] += _dot_nt(e_sc[r], (v2e if r % 2 == 0 else v2o)[kt])

    n_full = t0 // tk

    def full_tile(kt, carry):
        kv_tile(kt, False)
        return carry

    lax.fori_loop(0, n_full, full_tile, 0)
    kv_tile(n_full, True)
    def normalised(a, r):
        own = even if r % 2 == 0 else ~even
        l = a[:, hd:hd + 1] if r % 2 == 0 else a[:, 0:1]
        return jnp.where(own, a / jnp.maximum(l, 1e-30), 0.0)

    o_sel = [normalised(acc_sc[r], r) for r in range(REP_B)]

    wb = NSA_WB
    nwt = (WINDOW + wb) // tw
    o_win = [[] for _ in range(REP_B)]
    for hb in range(tq // wb):
        rows = slice(hb * wb, (hb + 1) * wb)
        w0 = jnp.maximum(t0 + hb * wb - WINDOW, 0)
        wt0 = w0 // tw
        wpos = w0 + lax.broadcasted_iota(I32, (1, nwt * tw), 1)
        for r in range(REP_B):
            for c in range(nwt):
                sw_sc[r, c] = _dot(qh[r][rows], kw2[wt0 + c])
        for r in range(REP_B):
            for rb in range(wb // rb_):
                rr = slice(rb * rb_, (rb + 1) * rb_)
                dist = tpos[hb * wb + rb * rb_:hb * wb + (rb + 1) * rb_] - wpos
                wmask = (dist >= 0) & (dist < WINDOW)
                sb = jnp.where(wmask, jnp.concatenate([sw_sc[r, c, rr] for c in range(nwt)], axis=1), NEG)
                e = jnp.exp(sb - jnp.max(sb, axis=-1, keepdims=True)).astype(ew_sc.dtype)
                for c in range(nwt):
                    ew_sc[r, c, rr] = e[:, c * tw:(c + 1) * tw]
        for r in range(REP_B):
            vw = vw2e if r % 2 == 0 else vw2o
            ow = _dot_nt(ew_sc[r, 0], vw[wt0])
            for c in range(1, nwt):
                ow = ow + _dot_nt(ew_sc[r, c], vw[wt0 + c])
            o_win[r].append(normalised(ow, r))
    o_win = [jnp.concatenate(parts, axis=0) for parts in o_win]

    for p in range(REP_B // 2):
        acc = jnp.zeros((tq, LANES), F32)
        for r in (2 * p, 2 * p + 1):
            acc = acc + gate(r, 0) * o_cmp[r] + gate(r, 1) * o_sel[r] + gate(r, 2) * o_win[r]
        o_ref[:, p * LANES:(p + 1) * LANES] = acc.astype(o_ref.dtype)


def nsa_prompt(q, kct, kvt_sel, kvt_win, gates, j, batch, n_cmp):
    m = q.shape[0]
    t = m // batch
    tq, tk, tw = NSA_TQ, NSA_TK, NSA_TW
    hd = HD_B
    n_sel = t // SEL_BLOCK
    ncp = kct.shape[3]
    nqt = t // tq
    ov = jnp.asarray(_overlap_matrix(ncp, LANES, n_cmp, n_sel), BF16)
    kb = np.arange(t) // SEL_BLOCK
    eb = np.where(np.arange(hd)[:, None] == kb[None, :], NEG_MASK, 0.0).astype(np.float32)
    eb = jnp.asarray(eb.reshape(hd, t // tk, tk).transpose(1, 0, 2), BF16)
    n_ab = kvt_sel.shape[0]
    kct4 = kct.reshape(batch, 2 * N_KV_B, hd, ncp)
    sel5 = kvt_sel.reshape(n_ab, batch, 2 * N_KV_B, hd, t)
    win5 = kvt_win.reshape(n_ab, batch, 2 * N_KV_B, hd, t)
    cspec = lambda kv: pl.BlockSpec((None, None, hd, ncp), lambda b, g, i: (b, kv * N_KV_B + g, 0, 0))
    rspec = lambda kv: pl.BlockSpec((None, None, None, hd, t), lambda b, g, i: (j, b, kv * N_KV_B + g, 0, 0))
    slab = lambda n, w: pltpu.VMEM((n, 2 * hd, w), BF16)
    return pl.pallas_call(
        functools.partial(_nsa_prompt_body, n_cmp, n_sel),
        grid=(batch, N_KV_B, nqt),
        in_specs=[pl.BlockSpec((tq, REP_B * hd), lambda b, g, i: (b * nqt + i, g)),
                  cspec(0), cspec(1), rspec(0), rspec(1), rspec(0), rspec(1),
                  pl.BlockSpec((tq, LANES), lambda b, g, i: (b * nqt + i, 0)),
                  _const_spec(ov.shape), _const_spec(eb.shape)],
        out_specs=pl.BlockSpec((tq, REP_B * hd), lambda b, g, i: (b * nqt + i, g)),
        out_shape=jax.ShapeDtypeStruct((m, N_KV_B * REP_B * hd), BF16),
        scratch_shapes=[pltpu.VMEM((2 * hd, ncp), BF16), pltpu.VMEM((2 * hd, ncp), BF16), pltpu.VMEM((2 * hd, ncp), BF16),
                        slab(t // tk, tk), slab(t // tk, tk), slab(t // tk, tk), slab(t // tk, tk),
                        slab(t // tw, tw), slab(t // tw, tw), slab(t // tw, tw),
                        pltpu.VMEM((REP_B, tq, tk), F32), pltpu.VMEM((REP_B, tq, tk), BF16),
                        pltpu.VMEM((REP_B, tq, 1), F32), pltpu.VMEM((REP_B, tq, LANES), F32),
                        pltpu.VMEM((REP_B, (WINDOW + NSA_WB) // tw, NSA_WB, tw), F32),
                        pltpu.VMEM((REP_B, (WINDOW + NSA_WB) // tw, NSA_WB, tw), BF16),
                        pltpu.VMEM((n_sel, tq), I32)],
        compiler_params=_cparams(("parallel", "parallel", "arbitrary"), 56),
        name="nsa_prompt",
    )(q, kct4, kct4, sel5, sel5, win5, win5, gates, ov, eb)


def _group_queries(q8, g):
    z = jnp.zeros_like(q8)
    return jnp.concatenate([q8, z] if g == 0 else [z, q8], axis=1)


def _nsa_step_select_body(q_pos, n_cmp, n_sel, q_ref, kct_ref, ov_ref, ocmp_ref, idx_ref):
    q8 = q_ref[...] * ATT_SCALE
    ncp = kct_ref.shape[2]
    nsp = ov_ref.shape[1]
    n_idx = lax.broadcasted_iota(I32, (1, ncp), 1)
    cmask = ((n_idx * CMP_STRIDE + (CMP_BLOCK - 1)) <= q_pos) & (n_idx < n_cmp)
    head = lax.broadcasted_iota(I32, (N_KV_B * REP_B, 1), 0)
    kct = kct_ref[0].astype(BF16)
    vct = kct_ref[1].astype(BF16)
    o_all = jnp.zeros((N_KV_B * REP_B, HD_B), F32)
    for g in range(N_KV_B):
        ingrp = (head // REP_B) == g
        p = _softmax_masked(_dot(_group_queries(q8, g), kct), cmask)
        o_g = _dot_nt(p.astype(BF16), vct)
        o_all = jnp.where(ingrp, o_g[:, g * HD_B:(g + 1) * HD_B], o_all)
        imp8 = _dot_01_rhs(jnp.where(ingrp, p, 0.0), ov_ref[...])
        imp = jnp.sum(imp8, axis=0, keepdims=True)
        lane_j = lax.broadcasted_iota(I32, (nsp, nsp), 1)
        sub_i = lax.broadcasted_iota(I32, (nsp, nsp), 0)
        cur = q_pos // SEL_BLOCK
        rank_row = jnp.broadcast_to(_rank(imp, lane_j[0:1], cur), (nsp, nsp))
        rank_col = rank_row.T
        before = (rank_row > rank_col) | ((rank_row == rank_col) & (lane_j < sub_i))
        order = jnp.sum(before.astype(F32), axis=1, keepdims=True)
        slot = lax.broadcasted_iota(I32, (nsp, LANES), 1).astype(F32)
        hit = (order == slot) & (sub_i[:, 0:1] < n_sel)
        blk = lax.broadcasted_iota(I32, (nsp, LANES), 0).astype(F32)
        idx = jnp.sum(jnp.where(hit, blk, 0.0), axis=0, keepdims=True)
        idx_ref[g:g + 1, :] = idx.astype(I32)
    ocmp_ref[...] = o_all


def nsa_step_select(q, kct, n_cmp, n_sel, q_pos):
    nb = q.shape[0]
    ncp = kct.shape[3]
    nsp = 256
    nh = N_KV_B * REP_B
    ov = jnp.asarray(_overlap_matrix(ncp, nsp, n_cmp, n_sel), BF16)
    return pl.pallas_call(
        functools.partial(_nsa_step_select_body, q_pos, n_cmp, n_sel),
        grid=(nb,),
        in_specs=[pl.BlockSpec((None, nh, HD_B), lambda b: (b, 0, 0)),
                  pl.BlockSpec((None, 2, KV_COLS // 2, ncp), lambda b: (b, 0, 0, 0)),
                  _const_spec(ov.shape)],
        out_specs=[pl.BlockSpec((None, nh, HD_B), lambda b: (b, 0, 0)),
                   pl.BlockSpec((None, N_KV_B, LANES), lambda b: (b, 0, 0))],
        out_shape=[jax.ShapeDtypeStruct((nb, nh, HD_B), F32),
                   jax.ShapeDtypeStruct((nb, N_KV_B, LANES), I32)],
        compiler_params=_cparams(("parallel",)),
        name="nsa_step_select",
    )(q, kct, ov)


def _attend_with_new(qp, kt, vt, kmask, k_new, v_new, new_ok):
    s = _dot(qp, kt)
    s_new = jnp.sum(qp.astype(F32) * k_new, axis=1, keepdims=True)
    sm = jnp.where(kmask, s, NEG)
    sn = jnp.where(new_ok, s_new, NEG)
    m = jnp.maximum(jnp.max(sm, axis=-1, keepdims=True), sn)
    e = jnp.where(kmask, jnp.exp(sm - m), 0.0)
    e_new = jnp.where(new_ok, jnp.exp(sn - m), 0.0)
    l = jnp.sum(e, axis=-1, keepdims=True) + e_new
    return (_dot_nt(e.astype(BF16), vt) + e_new * v_new) / jnp.maximum(l, 1e-30)


def _nsa_step_attend_body(q_pos, past_len, k_top, *refs):
    tbl_ref, idx_ref = refs[0], refs[1]
    pages = refs[2:2 + N_KV_B * k_top]
    (q_ref, ksn_ref, win_ref, kwn_ref, ocmp_ref, gt_ref, o_ref, kt_sc, vt_sc) = refs[2 + N_KV_B * k_top:]
    b = pl.program_id(0)
    nh = N_KV_B * REP_B
    half = KV_COLS // 2
    q8 = q_ref[...] * ATT_SCALE
    head = lax.broadcasted_iota(I32, (nh, 1), 0)
    nkeys = k_top * PAGE_SIZE
    wlen = win_ref.shape[2]
    wrow = lax.broadcasted_iota(I32, (1, wlen), 1)
    wdist = q_pos - (past_len - wlen + wrow)
    wmask = (wdist >= 0) & (wdist < WINDOW)
    kwt = win_ref[0].astype(BF16)
    vwt = win_ref[1].astype(BF16)
    ks_new, vs_new = ksn_ref[:, 0:half], ksn_ref[:, half:KV_COLS]
    kw_new, vw_new = kwn_ref[:, 0:half], kwn_ref[:, half:KV_COLS]
    o_sel = jnp.zeros((nh, HD_B), F32)
    o_win = jnp.zeros((nh, HD_B), F32)
    for g in range(N_KV_B):
        ingrp = (head // REP_B) == g
        qp = _group_queries(q8, g)
        has_new = jnp.zeros((), I32)
        krow = lax.broadcasted_iota(I32, (1, nkeys), 1)
        kmask = jnp.zeros((1, nkeys), jnp.bool_)
        for i in range(k_top):
            pg = pages[g * k_top + i]
            kt_sc[:, i * PAGE_SIZE:(i + 1) * PAGE_SIZE] = pg[0].astype(BF16)
            vt_sc[:, i * PAGE_SIZE:(i + 1) * PAGE_SIZE] = pg[1].astype(BF16)
            jb = idx_ref[b, g, i]
            kpos = (jb * SEL_BLOCK // PAGE_SIZE) * PAGE_SIZE + (krow - i * PAGE_SIZE)
            inblk = (krow >= i * PAGE_SIZE) & (krow < (i + 1) * PAGE_SIZE) & (kpos // SEL_BLOCK == jb)
            kmask = kmask | (inblk & (kpos <= q_pos) & (kpos < past_len))
            has_new = has_new | (jb * SEL_BLOCK + SEL_BLOCK > past_len).astype(I32)
        os_ = _attend_with_new(qp, kt_sc[...], vt_sc[...], kmask, ks_new, vs_new, has_new > 0)
        o_sel = jnp.where(ingrp, os_[:, g * HD_B:(g + 1) * HD_B], o_sel)
        ow_ = _attend_with_new(qp, kwt, vwt, wmask, kw_new, vw_new, True)
        o_win = jnp.where(ingrp, ow_[:, g * HD_B:(g + 1) * HD_B], o_win)
    gt = _sigmoid(gt_ref[...])
    o_ref[...] = (gt[:, 0:1] * ocmp_ref[...] + gt[:, 1:2] * o_sel + gt[:, 2:3] * o_win).astype(o_ref.dtype)


def nsa_step_attend(table, idx, cache_sel_t, win_t, j, q, kvs_new, kvw_new, o_cmp, gates, q_pos, past_len):
    nb = q.shape[0]
    k_top = N_SEL
    wlen = win_t.shape[4]
    half = KV_COLS // 2
    per_page = PAGE_SIZE // SEL_BLOCK
    last_cached = past_len // SEL_BLOCK - 1

    def page_map(b, t, ix, g, i):
        bb = jnp.minimum(b, nb - 1)
        jb = jnp.minimum(ix[bb, g, i], last_cached)
        return (j, t[bb, jb // per_page], 0, 0, 0)

    page_specs = [pl.BlockSpec((None, None, 2, half, PAGE_SIZE), functools.partial(page_map, g=g, i=i))
                  for g in range(N_KV_B) for i in range(k_top)]
    nh = N_KV_B * REP_B
    row_spec = lambda w: pl.BlockSpec((None, nh, w), lambda b, t, ix: (b, 0, 0))
    new_spec = pl.BlockSpec((None, 1, KV_COLS), lambda b, t, ix: (b, 0, 0))
    grid_spec = pltpu.PrefetchScalarGridSpec(
        num_scalar_prefetch=2,
        grid=(nb,),
        in_specs=page_specs + [row_spec(HD_B), new_spec,
                               pl.BlockSpec((None, None, 2, half, wlen), lambda b, t, ix: (j, b, 0, 0, 0)),
                               new_spec, row_spec(HD_B), row_spec(3)],
        out_specs=row_spec(HD_B),
        scratch_shapes=[pltpu.VMEM((half, k_top * PAGE_SIZE), BF16),
                        pltpu.VMEM((half, k_top * PAGE_SIZE), BF16)],
    )
    return pl.pallas_call(
        functools.partial(_nsa_step_attend_body, q_pos, past_len, k_top),
        grid_spec=grid_spec,
        out_shape=jax.ShapeDtypeStruct((nb, nh, HD_B), BF16),
        compiler_params=_cparams(("arbitrary",)),
        name="nsa_step_attend",
    )(table, idx, *([cache_sel_t] * (N_KV_B * k_top)), q, kvs_new, win_t, kvw_new, o_cmp, gates)


def _out_proj_body(oa_ref, ob_ref, w_ref, r_ref, o_ref):
    o_ref[...] = r_ref[...] + _dot(oa_ref[...], w_ref[0:D_A, :]) + _dot(ob_ref[...], w_ref[D_A:D_A + D_B, :])


def out_proj(o_a, o_b, w, resid):
    m = resid.shape[0]
    tm = min(m, 512)
    return pl.pallas_call(
        _out_proj_body,
        grid=(m // tm,),
        in_specs=[pl.BlockSpec((tm, D_A), lambda i: (i, 0)), pl.BlockSpec((tm, D_B), lambda i: (i, 0)),
                  _const_spec(w.shape), pl.BlockSpec((tm, D_MODEL), lambda i: (i, 0))],
        out_specs=pl.BlockSpec((tm, D_MODEL), lambda i: (i, 0)),
        out_shape=jax.ShapeDtypeStruct((m, D_MODEL), F32),
        compiler_params=_cparams(("parallel",)),
        name="out_proj",
    )(o_a, o_b, w, resid)


def _ffn_body(final, x_ref, g_ref, w1_ref, w2_ref, gf_ref, o_ref, xn_sc, acc_sc):
    jf = pl.program_id(1)

    @pl.when(jf == 0)
    def _():
        xn_sc[...] = _rms(x_ref[...], g_ref[...]).astype(BF16)
        acc_sc[...] = jnp.zeros(acc_sc.shape, F32)

    h = jnp.maximum(_dot(xn_sc[...], w1_ref[...]), 0.0)
    acc_sc[...] += _dot((h * h).astype(BF16), w2_ref[...])

    @pl.when(jf == pl.num_programs(1) - 1)
    def _():
        y = x_ref[...] + acc_sc[...]
        o_ref[...] = _rms(y, gf_ref[...]) if final else y


def ffn(x, g, w1, w2, g_final, final):
    m = x.shape[0]
    tm = min(m, 1024)
    tf = 1024
    return pl.pallas_call(
        functools.partial(_ffn_body, final),
        grid=(m // tm, D_FF // tf),
        in_specs=[pl.BlockSpec((tm, D_MODEL), lambda i, j: (i, 0)),
                  _const_spec((1, D_MODEL)),
                  pl.BlockSpec((D_MODEL, tf), lambda i, j: (0, j)),
                  pl.BlockSpec((tf, D_MODEL), lambda i, j: (j, 0)),
                  _const_spec((1, D_MODEL))],
        out_specs=pl.BlockSpec((tm, D_MODEL), lambda i, j: (i, 0)),
        out_shape=jax.ShapeDtypeStruct((m, D_MODEL), F32),
        scratch_shapes=[pltpu.VMEM((tm, D_MODEL), BF16), pltpu.VMEM((tm, D_MODEL), F32)],
        compiler_params=_cparams(("parallel", "arbitrary"), 48),
        name="ffn",
    )(x, g.reshape(1, D_MODEL), w1, w2, g_final.reshape(1, D_MODEL))


def _layernorm(v, g, b):
    mu = jnp.mean(v, axis=-1, keepdims=True)
    var = jnp.mean(jnp.square(v - mu), axis=-1, keepdims=True)
    return (v - mu) * lax.rsqrt(var + EPS) * g + b


def _gmlp_prompt_body(tm, x_ref, g_ref, win_ref, lng_ref, lnb_ref, ws_ref, bs_ref, wout_ref, o_ref, mix_sc):
    x = x_ref[...]
    z = _gelu(_dot(_rms(x, g_ref[...]).astype(BF16), win_ref[...]))
    u = z[:, 0:D_MODEL]
    v = _layernorm(z[:, D_MODEL:2 * D_MODEL], lng_ref[...], lnb_ref[...]).astype(BF16)
    row = lax.broadcasted_iota(I32, (CHUNK_C, CHUNK_C), 0)
    col = lax.broadcasted_iota(I32, (CHUNK_C, CHUNK_C), 1)
    gcd = D_MODEL // G_C
    for gi in range(G_C):
        ws = jnp.where(row >= col, ws_ref[gi], 0.0).astype(BF16)
        for ch in range(tm // CHUNK_C):
            rows = slice(ch * CHUNK_C, (ch + 1) * CHUNK_C)
            cols = slice(gi * gcd, (gi + 1) * gcd)
            mix_sc[rows, cols] = _dot(ws, v[rows, cols]) + bs_ref[gi]
    o_ref[...] = x + _dot((u * mix_sc[...]).astype(BF16), wout_ref[...])


def gmlp_prompt(x, g, w_in, ln_g, ln_b, w_s, b_s, w_out):
    m = x.shape[0]
    tm = 256
    return pl.pallas_call(
        functools.partial(_gmlp_prompt_body, tm),
        grid=(m // tm,),
        in_specs=[pl.BlockSpec((tm, D_MODEL), lambda i: (i, 0)),
                  _const_spec((1, D_MODEL)), _const_spec(w_in.shape),
                  _const_spec((1, D_MODEL)), _const_spec((1, D_MODEL)),
                  _const_spec(w_s.shape), _const_spec((G_C, CHUNK_C, 1)), _const_spec(w_out.shape)],
        out_specs=pl.BlockSpec((tm, D_MODEL), lambda i: (i, 0)),
        out_shape=jax.ShapeDtypeStruct((m, D_MODEL), F32),
        scratch_shapes=[pltpu.VMEM((tm, D_MODEL), F32)],
        compiler_params=_cparams(("parallel",), 48),
        name="gmlp_prompt",
    )(x, g.reshape(1, D_MODEL), w_in, ln_g.reshape(1, D_MODEL), ln_b.reshape(1, D_MODEL),
      w_s, b_s.reshape(G_C, CHUNK_C, 1), w_out)


def _gmlp_step_body(x_ref, g_ref, win_ref, lng_ref, lnb_ref, sc_ref, bi_ref, wout_ref, o_ref, v_ref):
    x = x_ref[...]
    z = _gelu(_dot(_rms(x, g_ref[...]).astype(BF16), win_ref[...]))
    v = _layernorm(z[:, D_MODEL:2 * D_MODEL], lng_ref[...], lnb_ref[...])
    v_ref[...] = v
    mix = v * sc_ref[...] + bi_ref[...]
    o_ref[...] = x + _dot((z[:, 0:D_MODEL] * mix).astype(BF16), wout_ref[...])


def gmlp_step(x, g, w_in, ln_g, ln_b, w_s, b_s, w_out):
    m = x.shape[0]
    gcd = D_MODEL // G_C
    scale = jnp.repeat(w_s[:, 0, 0], gcd).reshape(1, D_MODEL)
    bias = jnp.repeat(b_s[:, 0], gcd).reshape(1, D_MODEL)
    row = _const_spec((1, D_MODEL))
    full = _const_spec((m, D_MODEL))
    return pl.pallas_call(
        _gmlp_step_body,
        grid=(1,),
        in_specs=[full, row, _const_spec(w_in.shape), row, row, row, row, _const_spec(w_out.shape)],
        out_specs=[full, full],
        out_shape=[jax.ShapeDtypeStruct((m, D_MODEL), F32), jax.ShapeDtypeStruct((m, D_MODEL), F32)],
        compiler_params=_cparams(("arbitrary",), 48),
        name="gmlp_step",
    )(x, g.reshape(1, D_MODEL), w_in, ln_g.reshape(1, D_MODEL), ln_b.reshape(1, D_MODEL), scale, bias, w_out)


def _rows_last(a):
    nd = a.ndim
    perm = tuple(range(nd - 4)) + (nd - 3, nd - 2, nd - 1, nd - 4)
    a = a.transpose(perm)
    return a.reshape(a.shape[:nd - 3] + (N_KV_B * HD_B, a.shape[-1]))


def kernel(x_prompt, x_sample, cache_cmp_kv, cache_sel_kv, state_win_kv, state_hgrn, page_table, norm_mix, norm_ffn, norm_final, w_in_ab, w_out_ab, hgrn_lower_bounds, hgrn_norm, cmp_pe, cmp_w1, cmp_w2, w_in_c, ln_c_g, ln_c_b, w_s, b_s, w_out_c, w_ffn1, w_ffn2):
    bp, seq, d = x_prompt.shape
    bs = x_sample.shape[0]
    depth = norm_mix.shape[0]
    n_ab = w_in_ab.shape[0]
    npg = page_table.shape[1]
    past_len = npg * PAGE_SIZE
    hp = x_prompt.reshape(bp * seq, d)
    hs = x_sample.reshape(bs, d)

    prompt_pages = jnp.broadcast_to(jnp.arange(seq // PAGE_SIZE, dtype=I32), (bp, seq // PAGE_SIZE))
    n_cmp_p = seq // CMP_STRIDE - 1
    n_cmp_s = (past_len + 1) // CMP_STRIDE - 1
    n_sel_s = -(-(past_len + 1) // SEL_BLOCK)
    cache_cmp_t = _rows_last(cache_cmp_kv)
    cache_sel_t = _rows_last(cache_sel_kv)
    win_t = _rows_last(state_win_kv)
    kv_ofs = AB_SEGS[2][0]

    kvt_p = None
    cmp_s, sel_s, win_s, hg_p, hg_s, cv_s = [], [], [], [], [], []
    for layer in range(depth):
        j = layer // 2
        if layer % 2 == 0:
            w_full = w_in_ab[j]
            w_in = jnp.pad(w_full, ((0, 0), (0, AB_COLS_PAD - AB_COLS))).astype(BF16)
            w_zqg = jnp.concatenate([w_in[:, :kv_ofs], w_in[:, kv_ofs + 3 * KV_COLS:]], axis=1)
            w_kv_t = w_full[:, kv_ofs:kv_ofs + 3 * KV_COLS].T.astype(BF16)
            w_out = w_out_ab[j].astype(BF16)
            w1x, pex, w2xt = _compress_weights(cmp_pe[j], cmp_w1[j], cmp_w2[j])

            za, q, gt, kc_t, ks_t, kw_t = in_proj_ab_t(hp, norm_mix[layer], w_zqg, w_kv_t, j, n_ab, bp, kvt_p)
            kvt_p = (kc_t, ks_t, kw_t)
            o_a, s_fin = hgrn_prompt(za, hgrn_lower_bounds, hgrn_norm[j], j, bp)
            kct = compress_pages(prompt_pages, kc_t.reshape(n_ab, bp, 2, KV_COLS // 2, seq), j, True,
                                 w1x, pex, w2xt, npp=min(16, seq // PAGE_SIZE))
            o_b = nsa_prompt(q, kct, ks_t, kw_t, gt, j, bp, n_cmp_p)
            hp = out_proj(o_a, o_b, w_out, hp)
            hg_p.append(s_fin)

            za, q, kvc, kvs, kvw, gt = in_proj_ab(hs, norm_mix[layer], w_in)
            o_a, s_new = hgrn_step(za, state_hgrn, hgrn_lower_bounds, hgrn_norm[j], j)
            kct = compress_pages(page_table, cache_cmp_t, j, False, w1x, pex, w2xt, npp=min(16, npg))
            q8 = q.reshape(bs, N_KV_B * REP_B, HD_B)
            o_cmp, idx = nsa_step_select(q8, kct, n_cmp_s, n_sel_s, past_len)
            o_b = nsa_step_attend(page_table, idx, cache_sel_t, win_t, j, q8, kvs.reshape(bs, 1, KV_COLS),
                                  kvw.reshape(bs, 1, KV_COLS), o_cmp,
                                  gt[:, :3 * N_KV_B * REP_B].reshape(bs, N_KV_B * REP_B, 3), past_len, past_len)
            hs = out_proj(o_a, o_b.reshape(bs, D_B), w_out, hs)
            kv6s = lambda a: a.reshape(bs, 1, 2, N_KV_B, HD_B)
            cmp_s.append(kv6s(kvc)); sel_s.append(kv6s(kvs)); hg_s.append(s_new)
            win_s.append(jnp.concatenate([state_win_kv[j], kv6s(kvw)], axis=1)[:, 1:])
        else:
            w_in = w_in_c[j].astype(BF16)
            w_out = w_out_c[j].astype(BF16)
            hp = gmlp_prompt(hp, norm_mix[layer], w_in, ln_c_g[j], ln_c_b[j], w_s[j], b_s[j], w_out)
            hs, v_new = gmlp_step(hs, norm_mix[layer], w_in, ln_c_g[j], ln_c_b[j], w_s[j], b_s[j], w_out)
            cv_s.append(v_new.reshape(bs, 1, d))
        w1 = w_ffn1[layer].astype(BF16)
        w2 = w_ffn2[layer].astype(BF16)
        final = layer == depth - 1
        hp = ffn(hp, norm_ffn[layer], w1, w2, norm_final, final)
        hs = ffn(hs, norm_ffn[layer], w1, w2, norm_final, final)

    rows_first = lambda a: a.reshape(n_ab, bp, 2, N_KV_B, HD_B, seq).transpose(0, 1, 5, 2, 3, 4)
    cmp_p, sel_p, win_p = (rows_first(a) for a in kvt_p)
    return (hp.reshape(bp, seq, d), hs.reshape(bs, 1, d), cmp_p, jnp.stack(cmp_s),
            sel_p, jnp.stack(sel_s), win_p[:, :, seq - min(WINDOW, seq):], jnp.stack(win_s),
            jnp.stack(hg_p), jnp.stack(hg_s), jnp.stack(cv_s))
```
